```python
import math
import jax, jax.numpy as jnp
from jax import lax
import numpy as np

D_MODEL = 1024
BATCH = 16
SEQ = 2048
DEPTH = 1

D_MIX = 2 * D_MODEL
POOL_WINDOWS = (2, 4, 8, 16)
N_POOL_GROUPS = len(POOL_WINDOWS)
D_POOL = D_MIX // 4
POOL_GROUP = D_POOL // N_POOL_GROUPS
D_SSD = D_MIX - D_POOL
SSD_HEAD_DIM = 64
N_SSD_HEADS = D_SSD // SSD_HEAD_DIM
N_BC_GROUPS = 4
HEADS_PER_GROUP = N_SSD_HEADS // N_BC_GROUPS
D_STATE = 128
CONV_WIDTH = 4
CHUNK = 128
D_CONV = D_SSD + 2 * N_BC_GROUPS * D_STATE
D_IN_PROJ = D_POOL + D_SSD + D_CONV + N_SSD_HEADS
D_FF = 4 * D_MODEL
ALPHA = (2.0 * DEPTH) ** 0.25
BETA = (8.0 * DEPTH) ** -0.25
LN_EPS = 1e-5
RMS_EPS = 1e-5

kernel_name = "hybrid_pool_ssd_deepnorm_block"


def layer_norm(x, g, b):
    xf = x.astype(jnp.float32)
    mu = jnp.mean(xf, axis=-1, keepdims=True)
    var = jnp.mean(jnp.square(xf - mu), axis=-1, keepdims=True)
    return ((xf - mu) * lax.rsqrt(var + LN_EPS) * g + b).astype(x.dtype)


def pool_mixer(u, w_pool, pool_scale):
    b, s, _ = u.shape
    uf = u.astype(jnp.float32).reshape(b, s, N_POOL_GROUPS, POOL_GROUP)
    cs = jnp.pad(jnp.cumsum(uf, axis=1), ((0, 0), (1, 0), (0, 0), (0, 0)))
    pos = jnp.arange(1, s + 1)
    pooled = []
    for gi, w in enumerate(POOL_WINDOWS):
        c = cs[:, :, gi]
        lag = jnp.pad(c, ((0, 0), (w, 0), (0, 0)))[:, : s + 1]
        win_sum = (c - lag)[:, 1:]
        cnt = jnp.minimum(pos, w).astype(jnp.float32)
        pooled.append(win_sum / cnt[None, :, None])
    pooled = jnp.stack(pooled, axis=2)
    diff = (pooled - uf).astype(u.dtype)
    y = jnp.einsum("bsgc,gcd->bsgd", diff, w_pool)
    return y.reshape(b, s, D_POOL) * pool_scale


def causal_depthwise_conv(u, w, bias):
    ch = u.shape[-1]
    y = lax.conv_general_dilated(
        u, w[:, None, :].astype(u.dtype), window_strides=(1,),
        padding=[(CONV_WIDTH - 1, 0)], dimension_numbers=("NWC", "WIO", "NWC"),
        feature_group_count=ch)
    return y + bias


def ssd_chunked(x, dt, a_neg, bm, cm):
    b, s, h, p = x.shape
    nc = s // CHUNK
    g, r = N_BC_GROUPS, HEADS_PER_GROUP
    xdt = (x.astype(jnp.float32) * dt[..., None]).reshape(b, nc, CHUNK, g, r, p)
    bc = bm.astype(jnp.float32).reshape(b, nc, CHUNK, g, D_STATE)
    cc = cm.astype(jnp.float32).reshape(b, nc, CHUNK, g, D_STATE)
    a = (dt * a_neg).reshape(b, nc, CHUNK, g, r).transpose(0, 3, 4, 1, 2)
    a_cs = jnp.cumsum(a, axis=-1)
    causal = jnp.tril(jnp.ones((CHUNK, CHUNK), dtype=bool))
    seg = a_cs[..., :, None] - a_cs[..., None, :]
    decay = jnp.exp(jnp.where(causal, seg, -jnp.inf))
    cb = jnp.einsum("bclgn,bcsgn->bcgls", cc, bc)
    y_diag = jnp.einsum("bcgls,bgrcls,bcsgrp->bclgrp", cb, decay, xdt)
    decay_to_end = jnp.exp(a_cs[..., -1:] - a_cs)
    states = jnp.einsum("bclgn,bgrcl,bclgrp->bcgrpn", bc, decay_to_end, xdt)
    chunk_decay = jnp.exp(a_cs[..., -1])

    def step(carry, inp):
        st, dec = inp
        return carry * dec[..., None, None] + st, carry

    h0 = jnp.zeros((b, g, r, p, D_STATE), jnp.float32)
    _, prev = lax.scan(step, h0, (jnp.moveaxis(states, 1, 0), jnp.moveaxis(chunk_decay, -1, 0)))
    y_off = jnp.einsum("bclgn,cbgrpn,bgrcl->bclgrp", cc, prev, jnp.exp(a_cs))
    return (y_diag + y_off).reshape(b, s, h, p)


def gated_rmsnorm(y, z, w):
    v = y.astype(jnp.float32) * jax.nn.silu(z.astype(jnp.float32))
    lead = v.shape[:-1]
    vg = v.reshape(*lead, N_BC_GROUPS, D_SSD // N_BC_GROUPS)
    vg = vg * lax.rsqrt(jnp.mean(jnp.square(vg), axis=-1, keepdims=True) + RMS_EPS)
    return vg.reshape(*lead, D_SSD) * w


def mixer_sublayer(x, w_in, w_pool, pool_scale, conv_w, conv_b, dt_bias, a_log,
                   d_skip, ssd_norm_w, w_out):
    b, s, _ = x.shape
    proj = x @ w_in
    u_pool, z, xbc, dt_raw = jnp.split(
        proj, [D_POOL, D_POOL + D_SSD, D_POOL + D_SSD + D_CONV], axis=-1)
    y_pool = pool_mixer(u_pool, w_pool, pool_scale)
    xbc = jax.nn.silu(causal_depthwise_conv(xbc, conv_w, conv_b))
    xs, bm, cm = jnp.split(xbc, [D_SSD, D_SSD + N_BC_GROUPS * D_STATE], axis=-1)
    xs = xs.reshape(b, s, N_SSD_HEADS, SSD_HEAD_DIM)
    bm = bm.reshape(b, s, N_BC_GROUPS, D_STATE)
    cm = cm.reshape(b, s, N_BC_GROUPS, D_STATE)
    dt = jax.nn.softplus(dt_raw.astype(jnp.float32) + dt_bias)
    a_neg = -jnp.exp(a_log.astype(jnp.float32))
    y = ssd_chunked(xs, dt, a_neg, bm, cm) + d_skip[:, None] * xs
    y_ssd = gated_rmsnorm(y.reshape(b, s, D_SSD), z, ssd_norm_w)
    mixed = jnp.concatenate([y_pool.astype(x.dtype), y_ssd.astype(x.dtype)], axis=-1)
    return mixed @ w_out


def setup_inputs(seed: int = 0) -> dict:
    key = jax.random.key(seed)
    ks = jax.random.split(key, 17)
    f32 = jnp.float32
    L = DEPTH
    nrm = lambda k, shp: jax.random.normal(k, shp, f32)
    x = nrm(ks[0], (BATCH, SEQ, D_MODEL))
    w_in = nrm(ks[1], (L, D_MODEL, D_IN_PROJ)) * D_MODEL ** -0.5
    w_pool = nrm(ks[2], (L, N_POOL_GROUPS, POOL_GROUP, POOL_GROUP)) * POOL_GROUP ** -0.5
    pool_scale = 1.0 + 0.02 * nrm(ks[3], (L, D_POOL))
    bound = CONV_WIDTH ** -0.5
    conv_w = jax.random.uniform(ks[4], (L, CONV_WIDTH, D_CONV), f32, -bound, bound)
    conv_b = jax.random.uniform(ks[5], (L, D_CONV), f32, -bound, bound)
    dt0 = jnp.exp(jax.random.uniform(ks[6], (L, N_SSD_HEADS), f32, math.log(1e-3), math.log(1e-1)))
    dt_bias = dt0 + jnp.log(-jnp.expm1(-dt0))
    a_log = jnp.log(jax.random.uniform(ks[7], (L, N_SSD_HEADS), f32, 1.0, 16.0))
    d_skip = 1.0 + 0.1 * nrm(ks[8], (L, N_SSD_HEADS))
    ssd_norm_w = 1.0 + 0.02 * nrm(ks[9], (L, D_SSD))
    w_out = nrm(ks[10], (L, D_MIX, D_MODEL)) * (D_MIX ** -0.5) * BETA
    ln1_g = 1.0 + 0.02 * nrm(ks[11], (L, D_MODEL))
    ln1_b = 0.02 * nrm(ks[12], (L, D_MODEL))
    w_ff1 = nrm(ks[13], (L, D_MODEL, D_FF)) * D_MODEL ** -0.5
    w_ff2 = nrm(ks[14], (L, D_FF, D_MODEL)) * (D_FF ** -0.5) * BETA
    ln2_g = 1.0 + 0.02 * nrm(ks[15], (L, D_MODEL))
    ln2_b = 0.02 * nrm(ks[16], (L, D_MODEL))
    return {"x": x, "w_in": w_in, "w_pool": w_pool, "pool_scale": pool_scale,
            "conv_w": conv_w, "conv_b": conv_b, "dt_bias": dt_bias, "a_log": a_log,
            "d_skip": d_skip, "ssd_norm_w": ssd_norm_w, "w_out": w_out,
            "ln1_g": ln1_g, "ln1_b": ln1_b, "w_ff1": w_ff1, "w_ff2": w_ff2,
            "ln2_g": ln2_g, "ln2_b": ln2_b}


def reference(x, w_in, w_pool, pool_scale, conv_w, conv_b, dt_bias, a_log, d_skip,
              ssd_norm_w, w_out, ln1_g, ln1_b, w_ff1, w_ff2, ln2_g, ln2_b):
    h = x
    for l in range(DEPTH):
        mix = mixer_sublayer(h, w_in[l], w_pool[l], pool_scale[l], conv_w[l], conv_b[l],
                             dt_bias[l], a_log[l], d_skip[l], ssd_norm_w[l], w_out[l])
        h = layer_norm(ALPHA * h + mix.astype(h.dtype), ln1_g[l], ln1_b[l])
        ff = jnp.square(jax.nn.relu(h @ w_ff1[l])) @ w_ff2[l]
        h = layer_norm(ALPHA * h + ff.astype(h.dtype), ln2_g[l], ln2_b[l])
    return h
```

```python
import functools

import jax
import jax.numpy as jnp
from jax import lax
from jax.experimental import pallas as pl
from jax.experimental.pallas import tpu as pltpu

POOL_WINDOWS = (2, 4, 8, 16)
N_POOL_GROUPS = len(POOL_WINDOWS)
SSD_HEAD_DIM = 64
N_BC_GROUPS = 4
D_STATE = 128
CONV_WIDTH = 4
CHUNK = 128
LN_EPS = 1e-5
RMS_EPS = 1e-5

LANES = 128
CONV_HEAD = 8
POOL_HEAD = 16
DT_DUP_LANE = 32
HEADS_PER_SLAB = LANES // SSD_HEAD_DIM
VMEM_LIMIT_BYTES = 56 * 1024 * 1024

BF16 = jnp.bfloat16
F32 = jnp.float32


def _silu(v):
    return v * (1.0 / (1.0 + jnp.exp(-v)))


def _softplus(v):
    return jnp.maximum(v, 0.0) + jnp.log1p(jnp.exp(-jnp.abs(v)))


def _layer_norm(v, g, b):
    mu = jnp.mean(v, axis=-1, keepdims=True)
    d = v - mu
    var = jnp.mean(d * d, axis=-1, keepdims=True)
    return d * lax.rsqrt(var + LN_EPS) * g + b


def _dot(a, b):
    return jnp.dot(a, b, preferred_element_type=F32)


def _mixer_kernel(x_ref, wmain_ref, wdt_ref, convw_ref, convb_ref, dtb_ref, alog_ref, dskip_ref,
                  normw_ref, wpool_ref, pscale_ref, wout_ref, g_ref, b_ref, tri_ref,
                  h_ref,
                  ubuf, zbuf, xbuf, dtbuf, state, mixed,
                  *, ts, d_pool, d_ssd, n_heads, alpha):
    s_idx = pl.program_id(1)
    d_bc = N_BC_GROUPS * D_STATE
    heads_per_group = n_heads // N_BC_GROUPS
    slabs_per_group = heads_per_group // HEADS_PER_SLAB
    group_width = d_ssd // N_BC_GROUPS

    @pl.when(s_idx == 0)
    def _():
        ubuf[0:POOL_HEAD, :] = jnp.zeros((POOL_HEAD, d_pool), F32)
        xbuf[0:CONV_HEAD, :] = jnp.zeros((CONV_HEAD, xbuf.shape[1]), F32)
        state[...] = jnp.zeros(state.shape, F32)

    x = x_ref[0]
    xb = x.astype(BF16)
    ubuf[POOL_HEAD:POOL_HEAD + ts, :] = _dot(xb, wmain_ref[:, 0:d_pool])
    zbuf[...] = _dot(xb, wmain_ref[:, d_pool:d_pool + d_ssd])
    xbuf[CONV_HEAD:CONV_HEAD + ts, :] = _dot(xb, wmain_ref[:, d_pool + d_ssd:])
    dtbuf[...] = _softplus(_dot(xb, wdt_ref[...]) + dtb_ref[...])

    lane = lax.broadcasted_iota(jnp.int32, (1, LANES), 1)
    row_i = lax.broadcasted_iota(jnp.int32, (CHUNK, LANES), 0)
    col_i = lax.broadcasted_iota(jnp.int32, (CHUNK, LANES), 1)
    causal = row_i >= col_i
    lo_half = lane < SSD_HEAD_DIM
    a_neg = jnp.where(lane < n_heads, -jnp.exp(alog_ref[...]), 0.0)

    def conv_silu(r0, c0, width):
        win = xbuf[pl.ds(r0, CHUNK + CONV_HEAD), c0:c0 + width]
        w = convw_ref[:, c0:c0 + width]
        acc = convb_ref[:, c0:c0 + width] + win[CONV_HEAD:CONV_HEAD + CHUNK] * w[CONV_WIDTH - 1:CONV_WIDTH]
        for k in range(CONV_WIDTH - 1):
            off = CONV_HEAD - (CONV_WIDTH - 1) + k
            acc = acc + win[off:off + CHUNK] * w[k:k + 1]
        return _silu(acc)

    def chunk_body(c, carry):
        r0 = pl.multiple_of(c * CHUNK, CHUNK)

        uw = ubuf[pl.ds(r0, CHUNK + POOL_HEAD), :]
        pos = s_idx * ts + r0 + row_i + 1
        for gi, w in enumerate(POOL_WINDOWS):
            ug = uw[:, gi * LANES:(gi + 1) * LANES]
            cur = ug[POOL_HEAD:POOL_HEAD + CHUNK]
            win_sum = cur
            for j in range(1, w):
                win_sum = win_sum + ug[POOL_HEAD - j:POOL_HEAD - j + CHUNK]
            cnt = jnp.minimum(pos, w).astype(F32)
            diff = win_sum / cnt - cur
            yp = _dot(diff.astype(BF16), wpool_ref[gi]) * pscale_ref[:, gi * LANES:(gi + 1) * LANES]
            mixed[pl.ds(r0, CHUNK), gi * LANES:(gi + 1) * LANES] = yp.astype(BF16)

        dt_all = dtbuf[pl.ds(r0, CHUNK), :]
        a = dt_all * a_neg
        a_hi = a.astype(BF16)
        a_lo = (a - a_hi.astype(F32)).astype(BF16)
        acs = _dot(tri_ref[...], jnp.concatenate([a_hi, a_lo], axis=0))
        packed = acs + jnp.where(lane >= DT_DUP_LANE, dt_all, 0.0)
        packed_t = packed.T
        acs_t = packed_t[0:DT_DUP_LANE]
        dt_t = packed_t[DT_DUP_LANE:2 * DT_DUP_LANE]
        w_t = jnp.exp(acs_t[:, CHUNK - 1:CHUNK] - acs_t) * dt_t

        for g in range(N_BC_GROUPS):
            b_g = conv_silu(r0, d_ssd + g * D_STATE, D_STATE)
            c_g = conv_silu(r0, d_ssd + d_bc + g * D_STATE, D_STATE)
            b_bf = b_g.astype(BF16)
            c_bf = c_g.astype(BF16)
            b_t = b_g.T
            cb = lax.dot_general(c_bf, b_bf, (((1,), (1,)), ((), ())), preferred_element_type=F32)
            gcol = g * group_width
            y_off = _dot(c_bf, state[:, gcol:gcol + group_width].astype(BF16))
            v_slabs = []
            ssq = jnp.zeros((CHUNK, 1), F32)
            for q in range(slabs_per_group):
                col0 = gcol + q * LANES
                h0 = g * heads_per_group + q * HEADS_PER_SLAB
                xq = conv_silu(r0, col0, LANES)
                xq_bf = xq.astype(BF16)
                zero = jnp.zeros_like(xq_bf)
                rhs = jnp.concatenate([jnp.where(lo_half, xq_bf, zero), jnp.where(lo_half, zero, xq_bf)], axis=0)
                m_parts, bw_parts, e_parts = [], [], []
                for h in (h0, h0 + 1):
                    bc = jnp.broadcast_to(acs[:, h:h + 1], (CHUNK, LANES))
                    seg = bc - acs_t[h:h + 1, :]
                    dec = jnp.exp(jnp.where(causal, seg, -jnp.inf))
                    m_parts.append((cb * dec * dt_t[h:h + 1, :]).astype(BF16))
                    bw_parts.append((b_t * w_t[h:h + 1, :]).astype(BF16))
                    e_parts.append(jnp.exp(bc))
                lhs = jnp.concatenate([jnp.concatenate(m_parts, axis=1), jnp.concatenate(bw_parts, axis=1)], axis=0)
                res = _dot(lhs, rhs)
                e_pair = jnp.where(lo_half, e_parts[0], e_parts[1])
                y = res[0:CHUNK] + y_off[:, q * LANES:(q + 1) * LANES] * e_pair + dskip_ref[:, col0:col0 + LANES] * xq
                state[:, col0:col0 + LANES] = state[:, col0:col0 + LANES] * e_pair[CHUNK - 1:CHUNK, :] + res[CHUNK:2 * CHUNK]
                v = y * _silu(zbuf[pl.ds(r0, CHUNK), col0:col0 + LANES])
                ssq = ssq + jnp.sum(v * v, axis=-1, keepdims=True)
                v_slabs.append(v)
            scale = lax.rsqrt(ssq * (1.0 / group_width) + RMS_EPS)
            for q in range(slabs_per_group):
                col0 = gcol + q * LANES
                vn = v_slabs[q] * scale * normw_ref[:, col0:col0 + LANES]
                mixed[pl.ds(r0, CHUNK), d_pool + col0:d_pool + col0 + LANES] = vn.astype(BF16)
        return carry

    lax.fori_loop(0, ts // CHUNK, chunk_body, 0)

    ubuf[0:POOL_HEAD, :] = ubuf[ts:ts + POOL_HEAD, :]
    xbuf[0:CONV_HEAD, :] = xbuf[ts:ts + CONV_HEAD, :]

    mix = _dot(mixed[...], wout_ref[...])
    h_ref[0] = _layer_norm(alpha * x + mix, g_ref[...], b_ref[...])


def _ffn_kernel(h_ref, w1_ref, w2_ref, g_ref, b_ref, o_ref, *, alpha, ff_block):
    h = h_ref[...]
    hb = h.astype(BF16)
    d_ff = w1_ref.shape[1]
    acc = alpha * h
    for j in range(d_ff // ff_block):
        hid = _dot(hb, w1_ref[:, j * ff_block:(j + 1) * ff_block])
        act = jnp.square(jnp.maximum(hid, 0.0)).astype(BF16)
        acc = acc + _dot(act, w2_ref[j * ff_block:(j + 1) * ff_block, :])
    o_ref[...] = _layer_norm(acc, g_ref[...], b_ref[...])


def _resident(shape):
    nd = len(shape)
    return pl.BlockSpec(shape, lambda *_: (0,) * nd, pipeline_mode=pl.Buffered(1))


def _mixer_layer(x, w_in, w_pool, pool_scale, conv_w, conv_b, dt_bias, a_log, d_skip, ssd_norm_w,
                 w_out, ln_g, ln_b, *, alpha, ts):
    bsz, seq, d_model = x.shape
    d_pool = w_pool.shape[0] * w_pool.shape[1]
    n_heads = dt_bias.shape[0]
    d_ssd = n_heads * SSD_HEAD_DIM
    d_conv = conv_w.shape[1]
    d_main = d_pool + d_ssd + d_conv
    d_mix = d_pool + d_ssd
    assert w_in.shape[1] == d_main + n_heads and d_conv == d_ssd + 2 * N_BC_GROUPS * D_STATE
    assert n_heads <= DT_DUP_LANE and DT_DUP_LANE + n_heads <= LANES
    assert seq % ts == 0 and ts % CHUNK == 0

    w_main = w_in[:, :d_main].astype(BF16)
    w_dt = w_in[:, d_main:]
    w_dt_pad = jnp.zeros((d_model, LANES), F32)
    w_dt_pad = w_dt_pad.at[:, :n_heads].set(w_dt).at[:, DT_DUP_LANE:DT_DUP_LANE + n_heads].set(w_dt).astype(BF16)
    dtb_pad = jnp.zeros((1, LANES), F32).at[0, :n_heads].set(dt_bias).at[0, DT_DUP_LANE:DT_DUP_LANE + n_heads].set(dt_bias)
    alog_pad = jnp.zeros((1, LANES), F32).at[0, :n_heads].set(a_log)
    dskip_row = jnp.repeat(d_skip, SSD_HEAD_DIM)[None, :]
    tri = (jnp.arange(CHUNK)[:, None] >= (jnp.arange(2 * CHUNK)[None, :] % CHUNK)).astype(BF16)

    kern = functools.partial(_mixer_kernel, ts=ts, d_pool=d_pool, d_ssd=d_ssd, n_heads=n_heads, alpha=alpha)
    row = lambda v: v[None, :]
    return pl.pallas_call(
        kern,
        grid=(bsz, seq // ts),
        in_specs=[
            pl.BlockSpec((1, ts, d_model), lambda b, s: (b, s, 0)),
            _resident((d_model, d_main)),
            _resident((d_model, LANES)),
            _resident((CONV_WIDTH, d_conv)),
            _resident((1, d_conv)),
            _resident((1, LANES)),
            _resident((1, LANES)),
            _resident((1, d_ssd)),
            _resident((1, d_ssd)),
            _resident(w_pool.shape),
            _resident((1, d_pool)),
            _resident((d_mix, d_model)),
            _resident((1, d_model)),
            _resident((1, d_model)),
            _resident((CHUNK, 2 * CHUNK)),
        ],
        out_specs=pl.BlockSpec((1, ts, d_model), lambda b, s: (b, s, 0)),
        out_shape=jax.ShapeDtypeStruct(x.shape, F32),
        scratch_shapes=[
            pltpu.VMEM((POOL_HEAD + ts, d_pool), F32),
            pltpu.VMEM((ts, d_ssd), F32),
            pltpu.VMEM((CONV_HEAD + ts, d_conv), F32),
            pltpu.VMEM((ts, LANES), F32),
            pltpu.VMEM((D_STATE, d_ssd), F32),
            pltpu.VMEM((ts, d_mix), BF16),
        ],
        compiler_params=pltpu.CompilerParams(
            dimension_semantics=("arbitrary", "arbitrary"), vmem_limit_bytes=VMEM_LIMIT_BYTES),
        name="mixer",
    )(x, w_main, w_dt_pad, conv_w, row(conv_b), dtb_pad, alog_pad, dskip_row, row(ssd_norm_w),
      w_pool.astype(BF16), row(pool_scale), w_out.astype(BF16), row(ln_g), row(ln_b), tri)


def _ffn_layer(h, w1, w2, ln_g, ln_b, *, alpha, tm, ff_block):
    n_tok, d_model = h.shape
    d_ff = w1.shape[1]
    assert n_tok % tm == 0 and d_ff % ff_block == 0
    kern = functools.partial(_ffn_kernel, alpha=alpha, ff_block=ff_block)
    return pl.pallas_call(
        kern,
        grid=(n_tok // tm,),
        in_specs=[
            pl.BlockSpec((tm, d_model), lambda i: (i, 0)),
            _resident((d_model, d_ff)),
            _resident((d_ff, d_model)),
            _resident((1, d_model)),
            _resident((1, d_model)),
        ],
        out_specs=pl.BlockSpec((tm, d_model), lambda i: (i, 0)),
        out_shape=jax.ShapeDtypeStruct(h.shape, F32),
        compiler_params=pltpu.CompilerParams(
            dimension_semantics=("arbitrary",), vmem_limit_bytes=VMEM_LIMIT_BYTES),
        name="ffn",
    )(h, w1.astype(BF16), w2.astype(BF16), ln_g[None, :], ln_b[None, :])


def kernel(x, w_in, w_pool, pool_scale, conv_w, conv_b, dt_bias, a_log, d_skip, ssd_norm_w, w_out,
           ln1_g, ln1_b, w_ff1, w_ff2, ln2_g, ln2_b):
    depth = w_in.shape[0]
    alpha = (2.0 * depth) ** 0.25
    bsz, seq, d_model = x.shape
    h = x
    for l in range(depth):
        h = _mixer_layer(h, w_in[l], w_pool[l], pool_scale[l], conv_w[l], conv_b[l], dt_bias[l], a_log[l],
                         d_skip[l], ssd_norm_w[l], w_out[l], ln1_g[l], ln1_b[l], alpha=alpha, ts=min(256, seq))
        h = _ffn_layer(h.reshape(bsz * seq, d_model), w_ff1[l], w_ff2[l], ln2_g[l], ln2_b[l],
                       alpha=alpha, tm=512, ff_block=1024).reshape(bsz, seq, d_model)
    return h
```

```python
import functools
import math

import jax
import jax.numpy as jnp
from jax import lax
from jax.experimental import pallas as pl
from jax.experimental.pallas import tpu as pltpu

POOL_WINDOWS = (2, 4, 8, 16)
N_POOL_GROUPS = len(POOL_WINDOWS)
SSD_HEAD_DIM = 64
N_BC_GROUPS = 4
D_STATE = 128
CONV_WIDTH = 4
CHUNK = 128
LN_EPS = 1e-5
RMS_EPS = 1e-5

LANES = 128
MXU_COLS = 256
CONV_HEAD = 8
POOL_HEAD = 16
DT_DUP_LANE = 32
HEADS_PER_SLAB = LANES // SSD_HEAD_DIM
VMEM_LIMIT_BYTES = 56 * 1024 * 1024
CHUNKS_PER_STEP = 2
FFN_ROWS = 512
FFN_COLS = 1024
LOG2E = math.log2(math.e)

BF16 = jnp.bfloat16
F32 = jnp.float32


def _silu(v):
    return v * (1.0 / (1.0 + jnp.exp2(v * (-LOG2E))))


def _softplus(v):
    return jnp.maximum(v, 0.0) + jnp.log1p(jnp.exp(-jnp.abs(v)))


def _layer_norm(v, g, b):
    mu = jnp.mean(v, axis=-1, keepdims=True)
    d = v - mu
    var = jnp.mean(d * d, axis=-1, keepdims=True)
    return d * lax.rsqrt(var + LN_EPS) * g + b


def _dot(a, b):
    return jnp.dot(a, b, preferred_element_type=F32)


class _Interleaver:
    def __init__(self, tasks, points):
        self._tasks = list(tasks)
        self._points = points

    def step(self):
        n = -(-len(self._tasks) // max(self._points, 1))
        self._points -= 1
        for task in self._tasks[:n]:
            task()
        del self._tasks[:n]

    def drain(self):
        for task in self._tasks:
            task()
        self._tasks = []


def _mixer_kernel(x_ref, xprev_ref, wmain_ref, wdt_ref, convw_ref, convb_ref, dtb_ref, alog_ref, dskip_ref,
                  normw_ref, wpool_ref, pscale_ref, wout_ref, g_ref, b_ref, tri_ref,
                  h_ref,
                  u0, z0, xc0, dt0, mixed0, u1, z1, xc1, dt1, mixed1, state, xb_ref, hpre_ref,
                  *, chunks_per_seq, d_pool, d_ssd, n_heads, alpha):
    t = pl.program_id(0)
    heads_per_group = n_heads // N_BC_GROUPS
    slabs_per_group = heads_per_group // HEADS_PER_SLAB
    group_width = d_ssd // N_BC_GROUPS
    x_slabs = d_ssd // LANES
    sets = ((u0, z0, xc0, dt0, mixed0), (u1, z1, xc1, dt1, mixed1))

    @pl.when(t == 0)
    def _():
        for ref in (u0, z0, xc0, dt0, mixed0, u1, z1, xc1, dt1, mixed1, state, xb_ref, hpre_ref):
            ref[...] = jnp.zeros(ref.shape, ref.dtype)

    lane = lax.broadcasted_iota(jnp.int32, (1, LANES), 1)
    row_i = lax.broadcasted_iota(jnp.int32, (CHUNK, LANES), 0)
    col_i = lax.broadcasted_iota(jnp.int32, (CHUNK, LANES), 1)
    causal = row_i >= col_i
    lo_half = lane < SSD_HEAD_DIM
    a_neg2 = jnp.where(lane < n_heads, -LOG2E * jnp.exp(alog_ref[...]), 0.0)

    def project_tasks(r0, dst):
        ubuf, zbuf, xbuf, dtbuf, _ = dst
        slabs_per_dot = MXU_COLS // LANES

        def cast():
            xb_ref[...] = x_ref[r0:r0 + CHUNK, :].astype(BF16)

        def slab_task(buf, col0, j0, head):
            def run():
                res = _dot(xb_ref[...], wmain_ref[:, col0 + j0 * LANES:col0 + (j0 + slabs_per_dot) * LANES])
                for dj in range(slabs_per_dot):
                    buf[j0 + dj, head:head + CHUNK, :] = res[:, dj * LANES:(dj + 1) * LANES]
            return run

        def gate_task(c0):
            def run():
                zbuf[:, c0:c0 + MXU_COLS] = _dot(xb_ref[...], wmain_ref[:, d_pool + c0:d_pool + c0 + MXU_COLS])
            return run

        def dt_task():
            dtbuf[...] = _softplus(_dot(xb_ref[...], wdt_ref[...]) + dtb_ref[...])

        tasks = [cast, dt_task]
        tasks += [slab_task(xbuf, d_pool + d_ssd, j0, CONV_HEAD) for j0 in range(0, xbuf.shape[0], slabs_per_dot)]
        tasks += [slab_task(ubuf, 0, j0, POOL_HEAD) for j0 in range(0, ubuf.shape[0], slabs_per_dot)]
        tasks += [gate_task(c0) for c0 in range(0, d_ssd, MXU_COLS)]
        return tasks

    def out_project_tasks(r0, src):
        d_model = wout_ref.shape[1]

        def block_task(c0):
            def run():
                hpre_ref[:, c0:c0 + MXU_COLS] = _dot(src[4][...], wout_ref[:, c0:c0 + MXU_COLS])
            return run

        def norm_task():
            pre = alpha * xprev_ref[r0:r0 + CHUNK, :] + hpre_ref[...]
            h_ref[r0:r0 + CHUNK, :] = _layer_norm(pre, g_ref[...], b_ref[...])

        return [block_task(c0) for c0 in range(0, d_model, MXU_COLS)] + [norm_task]

    def mix_chunk(chunk, cur, prev, fill):
        ubuf, zbuf, xbuf, dtbuf, mixed = cur
        seq_chunk = lax.rem(chunk + chunks_per_seq, chunks_per_seq)
        is_start = seq_chunk == 0
        ubuf[:, 0:POOL_HEAD, :] = jnp.where(is_start, 0.0, prev[0][:, CHUNK:CHUNK + POOL_HEAD, :])
        xbuf[:, 0:CONV_HEAD, :] = jnp.where(is_start, 0.0, prev[2][:, CHUNK:CHUNK + CONV_HEAD, :])

        def conv_silu(j):
            cols = slice(j * LANES, (j + 1) * LANES)
            acc = convb_ref[:, cols] + xbuf[j, CONV_HEAD:CONV_HEAD + CHUNK, :] * convw_ref[CONV_WIDTH - 1:CONV_WIDTH, cols]
            for k in range(CONV_WIDTH - 1):
                off = CONV_HEAD - (CONV_WIDTH - 1) + k
                acc = acc + xbuf[j, off:off + CHUNK, :] * convw_ref[k:k + 1, cols]
            return _silu(acc)

        pos = seq_chunk * CHUNK + row_i + 1
        for gi, w in enumerate(POOL_WINDOWS):
            cur_u = ubuf[gi, POOL_HEAD:POOL_HEAD + CHUNK, :]
            win_sum = cur_u
            for j in range(1, w):
                win_sum = win_sum + ubuf[gi, POOL_HEAD - j:POOL_HEAD - j + CHUNK, :]
            cnt = jnp.minimum(pos, w).astype(F32)
            diff = win_sum / cnt - cur_u
            yp = _dot(diff.astype(BF16), wpool_ref[gi]) * pscale_ref[:, gi * LANES:(gi + 1) * LANES]
            mixed[:, gi * LANES:(gi + 1) * LANES] = yp.astype(BF16)
            fill.step()

        dt_all = dtbuf[...]
        a = dt_all * a_neg2
        a_hi = a.astype(BF16)
        a_lo = (a - a_hi.astype(F32)).astype(BF16)
        acs = _dot(tri_ref[...], jnp.concatenate([a_hi, a_lo], axis=0))
        packed = acs + jnp.where(lane >= DT_DUP_LANE, dt_all, 0.0)
        packed_t = packed.T
        acs_t = packed_t[0:DT_DUP_LANE]
        dt_t = packed_t[DT_DUP_LANE:2 * DT_DUP_LANE]
        w_t = jnp.exp2(acs_t[:, CHUNK - 1:CHUNK] - acs_t) * dt_t
        src_t = acs_t - jnp.log2(dt_t)

        for g in range(N_BC_GROUPS):
            b_g = conv_silu(x_slabs + g)
            c_g = conv_silu(x_slabs + N_BC_GROUPS + g)
            b_bf = b_g.astype(BF16)
            c_bf = c_g.astype(BF16)
            b_t = b_g.T
            cb = lax.dot_general(c_bf, b_bf, (((1,), (1,)), ((), ())), preferred_element_type=F32)
            gcol = g * group_width
            st_g = jnp.where(is_start, 0.0, state[:, gcol:gcol + group_width])
            y_off = _dot(c_bf, st_g.astype(BF16))
            fill.step()
            v_slabs = []
            ssq = jnp.zeros((CHUNK, 1), F32)
            for q in range(slabs_per_group):
                col0 = gcol + q * LANES
                h0 = g * heads_per_group + q * HEADS_PER_SLAB
                xq = conv_silu(g * slabs_per_group + q)
                xq_bf = xq.astype(BF16)
                zero = jnp.zeros_like(xq_bf)
                rhs = jnp.concatenate([jnp.where(lo_half, xq_bf, zero), jnp.where(lo_half, zero, xq_bf)], axis=0)
                m_parts, bw_parts, e_parts = [], [], []
                for h in (h0, h0 + 1):
                    bc = jnp.broadcast_to(acs[:, h:h + 1], (CHUNK, LANES))
                    dec_dt = jnp.exp2(jnp.where(causal, bc - src_t[h:h + 1, :], -jnp.inf))
                    m_parts.append((cb * dec_dt).astype(BF16))
                    bw_parts.append((b_t * w_t[h:h + 1, :]).astype(BF16))
                    e_parts.append(jnp.exp2(bc))
                lhs = jnp.concatenate([jnp.concatenate(m_parts, axis=1), jnp.concatenate(bw_parts, axis=1)], axis=0)
                res = _dot(lhs, rhs)
                e_pair = jnp.where(lo_half, e_parts[0], e_parts[1])
                y = res[0:CHUNK] + y_off[:, q * LANES:(q + 1) * LANES] * e_pair + dskip_ref[:, col0:col0 + LANES] * xq
                state[:, col0:col0 + LANES] = (st_g[:, q * LANES:(q + 1) * LANES] * e_pair[CHUNK - 1:CHUNK, :]
                                               + res[CHUNK:2 * CHUNK])
                v = y * _silu(zbuf[:, col0:col0 + LANES])
                ssq = ssq + jnp.sum(v * v, axis=-1, keepdims=True)
                v_slabs.append(v)
                fill.step()
            scale = lax.rsqrt(ssq * (1.0 / group_width) + RMS_EPS)
            for q in range(slabs_per_group):
                col0 = gcol + q * LANES
                vn = v_slabs[q] * scale * normw_ref[:, col0:col0 + LANES]
                mixed[:, d_pool + col0:d_pool + col0 + LANES] = vn.astype(BF16)

    fill_points = N_POOL_GROUPS + N_BC_GROUPS * (1 + slabs_per_group)
    first = CHUNKS_PER_STEP * t
    fill = _Interleaver(out_project_tasks(0, sets[0]) + project_tasks(0, sets[0]), fill_points)
    mix_chunk(first - 1, sets[1], sets[0], fill)
    fill.drain()
    fill = _Interleaver(out_project_tasks(CHUNK, sets[1]) + project_tasks(CHUNK, sets[1]), fill_points)
    mix_chunk(first, sets[0], sets[1], fill)
    fill.drain()


def _ffn_kernel(h_ref, w1_ref, w2_ref, g_ref, b_ref, o_ref, *, alpha, ff_block):
    h = h_ref[...]
    hb = h.astype(BF16)
    d_ff = w1_ref.shape[1]
    acc = alpha * h
    for j in range(d_ff // ff_block):
        hid = _dot(hb, w1_ref[:, j * ff_block:(j + 1) * ff_block])
        act = jnp.square(jnp.maximum(hid, 0.0)).astype(BF16)
        acc = acc + _dot(act, w2_ref[j * ff_block:(j + 1) * ff_block, :])
    o_ref[...] = _layer_norm(acc, g_ref[...], b_ref[...])


def _resident(shape):
    nd = len(shape)
    return pl.BlockSpec(shape, lambda *_: (0,) * nd, pipeline_mode=pl.Buffered(1))


def _mixer_layer(x, w_in, w_pool, pool_scale, conv_w, conv_b, dt_bias, a_log, d_skip, ssd_norm_w,
                 w_out, ln_g, ln_b, *, alpha):
    bsz, seq, d_model = x.shape
    n_tok = bsz * seq
    step_rows = CHUNKS_PER_STEP * CHUNK
    n_blocks = n_tok // step_rows
    d_pool = w_pool.shape[0] * w_pool.shape[1]
    n_heads = dt_bias.shape[0]
    d_ssd = n_heads * SSD_HEAD_DIM
    d_conv = conv_w.shape[1]
    d_main = d_pool + d_ssd + d_conv
    d_mix = d_pool + d_ssd
    assert w_in.shape[1] == d_main + n_heads and d_conv == d_ssd + 2 * N_BC_GROUPS * D_STATE
    assert n_heads <= DT_DUP_LANE and DT_DUP_LANE + n_heads <= LANES
    assert seq % step_rows == 0 and w_pool.shape[1] == LANES

    x2d = x.reshape(n_tok, d_model)
    w_main = w_in[:, :d_main].astype(BF16)
    w_dt = w_in[:, d_main:]
    w_dt_pad = jnp.zeros((d_model, LANES), F32)
    w_dt_pad = w_dt_pad.at[:, :n_heads].set(w_dt).at[:, DT_DUP_LANE:DT_DUP_LANE + n_heads].set(w_dt).astype(BF16)
    dtb_pad = jnp.zeros((1, LANES), F32).at[0, :n_heads].set(dt_bias).at[0, DT_DUP_LANE:DT_DUP_LANE + n_heads].set(dt_bias)
    alog_pad = jnp.zeros((1, LANES), F32).at[0, :n_heads].set(a_log)
    dskip_row = jnp.repeat(d_skip, SSD_HEAD_DIM)[None, :]
    tri = (jnp.arange(CHUNK)[:, None] >= (jnp.arange(2 * CHUNK)[None, :] % CHUNK)).astype(BF16)

    kern = functools.partial(_mixer_kernel, chunks_per_seq=seq // CHUNK, d_pool=d_pool, d_ssd=d_ssd,
                             n_heads=n_heads, alpha=alpha)
    row = lambda v: v[None, :]

    def chunk_set():
        return [
            pltpu.VMEM((d_pool // LANES, POOL_HEAD + CHUNK, LANES), F32),
            pltpu.VMEM((CHUNK, d_ssd), F32),
            pltpu.VMEM((d_conv // LANES, CONV_HEAD + CHUNK, LANES), F32),
            pltpu.VMEM((CHUNK, LANES), F32),
            pltpu.VMEM((CHUNK, d_mix), BF16),
        ]

    out = pl.pallas_call(
        kern,
        grid=(n_blocks + 1,),
        in_specs=[
            pl.BlockSpec((step_rows, d_model), lambda t: (jnp.minimum(t, n_blocks - 1), 0)),
            pl.BlockSpec((step_rows, d_model), lambda t: (jnp.maximum(t - 1, 0), 0)),
            _resident((d_model, d_main)),
            _resident((d_model, LANES)),
            _resident((CONV_WIDTH, d_conv)),
            _resident((1, d_conv)),
            _resident((1, LANES)),
            _resident((1, LANES)),
            _resident((1, d_ssd)),
            _resident((1, d_ssd)),
            _resident(w_pool.shape),
            _resident((1, d_pool)),
            _resident((d_mix, d_model)),
            _resident((1, d_model)),
            _resident((1, d_model)),
            _resident((CHUNK, 2 * CHUNK)),
        ],
        out_specs=pl.BlockSpec((step_rows, d_model), lambda t: (jnp.maximum(t - 1, 0), 0)),
        out_shape=jax.ShapeDtypeStruct((n_tok, d_model), F32),
        scratch_shapes=chunk_set() + chunk_set() + [
            pltpu.VMEM((D_STATE, d_ssd), F32),
            pltpu.VMEM((CHUNK, d_model), BF16),
            pltpu.VMEM((CHUNK, d_model), F32),
        ],
        compiler_params=pltpu.CompilerParams(
            dimension_semantics=("arbitrary",), vmem_limit_bytes=VMEM_LIMIT_BYTES),
        name="mixer",
    )(x2d, x2d, w_main, w_dt_pad, conv_w, row(conv_b), dtb_pad, alog_pad, dskip_row, row(ssd_norm_w),
      w_pool.astype(BF16), row(pool_scale), w_out.astype(BF16), row(ln_g), row(ln_b), tri)
    return out


def _ffn_layer(h, w1, w2, ln_g, ln_b, *, alpha, tm, ff_block):
    n_tok, d_model = h.shape
    d_ff = w1.shape[1]
    assert n_tok % tm == 0 and d_ff % ff_block == 0
    kern = functools.partial(_ffn_kernel, alpha=alpha, ff_block=ff_block)
    return pl.pallas_call(
        kern,
        grid=(n_tok // tm,),
        in_specs=[
            pl.BlockSpec((tm, d_model), lambda i: (i, 0)),
            _resident((d_model, d_ff)),
            _resident((d_ff, d_model)),
            _resident((1, d_model)),
            _resident((1, d_model)),
        ],
        out_specs=pl.BlockSpec((tm, d_model), lambda i: (i, 0)),
        out_shape=jax.ShapeDtypeStruct(h.shape, F32),
        compiler_params=pltpu.CompilerParams(
            dimension_semantics=("arbitrary",), vmem_limit_bytes=VMEM_LIMIT_BYTES),
        name="ffn",
    )(h, w1.astype(BF16), w2.astype(BF16), ln_g[None, :], ln_b[None, :])


def kernel(x, w_in, w_pool, pool_scale, conv_w, conv_b, dt_bias, a_log, d_skip, ssd_norm_w, w_out,
           ln1_g, ln1_b, w_ff1, w_ff2, ln2_g, ln2_b):
    depth = w_in.shape[0]
    alpha = (2.0 * depth) ** 0.25
    bsz, seq, d_model = x.shape
    h = x
    for l in range(depth):
        h = _mixer_layer(h.reshape(bsz, seq, d_model), w_in[l], w_pool[l], pool_scale[l], conv_w[l], conv_b[l],
                         dt_bias[l], a_log[l], d_skip[l], ssd_norm_w[l], w_out[l], ln1_g[l], ln1_b[l], alpha=alpha)
        h = _ffn_layer(h, w_ff1[l], w_ff2[l], ln2_g[l], ln2_b[l], alpha=alpha, tm=FFN_ROWS, ff_block=FFN_COLS)
    return h.reshape(bsz, seq, d_model)
```

```python
import functools
import math

import jax
import jax.numpy as jnp
from jax import lax
from jax.experimental import pallas as pl
from jax.experimental.pallas import tpu as pltpu

POOL_WINDOWS = (2, 4, 8, 16)
N_POOL_GROUPS = len(POOL_WINDOWS)
SSD_HEAD_DIM = 64
N_BC_GROUPS = 4
D_STATE = 128
CONV_WIDTH = 4
CHUNK = 128
LN_EPS = 1e-5
RMS_EPS = 1e-5

LANES = 128
MXU_COLS = 256
CONV_HEAD = 8
POOL_HEAD = 16
DT_DUP_LANE = 32
HEADS_PER_SLAB = LANES // SSD_HEAD_DIM
VMEM_LIMIT_BYTES = 56 * 1024 * 1024
CHUNKS_PER_HALF = 2
CHUNKS_PER_STEP = 2 * CHUNKS_PER_HALF
FFN_ROWS = 1024
FFN_COLS = 1024
LOG2E = math.log2(math.e)

BF16 = jnp.bfloat16
F32 = jnp.float32


def _silu(v):
    return v * (1.0 / (1.0 + jnp.exp2(v * (-LOG2E))))


def _softplus(v):
    return jnp.maximum(v, 0.0) + jnp.log1p(jnp.exp(-jnp.abs(v)))


def _layer_norm(v, g, b):
    mu = jnp.mean(v, axis=-1, keepdims=True)
    d = v - mu
    var = jnp.mean(d * d, axis=-1, keepdims=True)
    return d * lax.rsqrt(var + LN_EPS) * g + b


def _dot(a, b):
    return jnp.dot(a, b, preferred_element_type=F32)


class _Interleaver:
    def __init__(self, tasks, points):
        self._tasks = list(tasks)
        self._total = len(self._tasks)
        self._points = points
        self._seen = 0
        self._done = 0

    def step(self):
        left = self._points - self._seen
        self._seen += 1
        target = self._done + -(-(self._total - self._done) // max(left, 1))
        while self._done < min(target, self._total):
            self._tasks[self._done]()
            self._done += 1

    def drain(self):
        while self._done < self._total:
            self._tasks[self._done]()
            self._done += 1


def _mixer_kernel(x_ref, xprev_ref, wmain_ref, wdt_ref, convw_ref, convb_ref, dtb_ref, alog_ref, dskip_ref,
                  normw_ref, wpool_ref, pscale_ref, wout_ref, g_ref, b_ref, tri_ref,
                  h_ref,
                  *scratch,
                  chunks_per_seq, d_pool, d_ssd, n_heads, alpha):
    t = pl.program_id(0)
    heads_per_group = n_heads // N_BC_GROUPS
    slabs_per_group = heads_per_group // HEADS_PER_SLAB
    group_width = d_ssd // N_BC_GROUPS
    x_slabs = d_ssd // LANES
    half_rows = CHUNKS_PER_HALF * CHUNK
    sets = [scratch[4 * i:4 * i + 4] for i in range(CHUNKS_PER_STEP)]
    mixed_refs = scratch[4 * CHUNKS_PER_STEP:4 * CHUNKS_PER_STEP + 2]
    state, xb_ref, hpre_ref = scratch[4 * CHUNKS_PER_STEP + 2:]

    @pl.when(t == 0)
    def _():
        for ref in scratch:
            ref[...] = jnp.zeros(ref.shape, ref.dtype)

    lane = lax.broadcasted_iota(jnp.int32, (1, LANES), 1)
    row_i = lax.broadcasted_iota(jnp.int32, (CHUNK, LANES), 0)
    col_i = lax.broadcasted_iota(jnp.int32, (CHUNK, LANES), 1)
    causal = row_i >= col_i
    lo_half = lane < SSD_HEAD_DIM
    a_neg2 = jnp.where(lane < n_heads, -LOG2E * jnp.exp(alog_ref[...]), 0.0)

    def project_tasks(r0, dsts):
        slabs_per_dot = MXU_COLS // LANES

        def cast():
            xb_ref[...] = x_ref[r0:r0 + half_rows, :].astype(BF16)

        def slab_task(which, col0, j0, head):
            def run():
                res = _dot(xb_ref[...], wmain_ref[:, col0 + j0 * LANES:col0 + (j0 + slabs_per_dot) * LANES])
                for ci, dst in enumerate(dsts):
                    for dj in range(slabs_per_dot):
                        dst[which][j0 + dj, head:head + CHUNK, :] = res[ci * CHUNK:(ci + 1) * CHUNK, dj * LANES:(dj + 1) * LANES]
            return run

        def gate_task(c0):
            def run():
                res = _dot(xb_ref[...], wmain_ref[:, d_pool + c0:d_pool + c0 + MXU_COLS])
                for ci, dst in enumerate(dsts):
                    dst[1][:, c0:c0 + MXU_COLS] = res[ci * CHUNK:(ci + 1) * CHUNK]
            return run

        def dt_task():
            res = _softplus(_dot(xb_ref[...], wdt_ref[...]) + dtb_ref[...])
            for ci, dst in enumerate(dsts):
                dst[3][...] = res[ci * CHUNK:(ci + 1) * CHUNK]

        n_conv_slabs, n_pool_slabs = dsts[0][2].shape[0], dsts[0][0].shape[0]
        tasks = [cast, dt_task]
        tasks += [slab_task(2, d_pool + d_ssd, j0, CONV_HEAD) for j0 in range(0, n_conv_slabs, slabs_per_dot)]
        tasks += [slab_task(0, 0, j0, POOL_HEAD) for j0 in range(0, n_pool_slabs, slabs_per_dot)]
        tasks += [gate_task(c0) for c0 in range(0, d_ssd, MXU_COLS)]
        return tasks

    def out_project_tasks(r0, mixed):
        d_model = wout_ref.shape[1]

        def block_task(c0):
            def run():
                hpre_ref[:, c0:c0 + MXU_COLS] = _dot(mixed[...], wout_ref[:, c0:c0 + MXU_COLS])
            return run

        def norm_task(ci):
            def run():
                rows = slice(ci * CHUNK, (ci + 1) * CHUNK)
                pre = alpha * xprev_ref[r0 + ci * CHUNK:r0 + (ci + 1) * CHUNK, :] + hpre_ref[rows, :]
                h_ref[r0 + ci * CHUNK:r0 + (ci + 1) * CHUNK, :] = _layer_norm(pre, g_ref[...], b_ref[...])
            return run

        return [block_task(c0) for c0 in range(0, d_model, MXU_COLS)] + [norm_task(ci) for ci in range(CHUNKS_PER_HALF)]

    def mix_chunk(chunk, cur, prev, mixed, m0, fill):
        ubuf, zbuf, xbuf, dtbuf = cur
        mrows = slice(m0, m0 + CHUNK)
        seq_chunk = lax.rem(chunk + chunks_per_seq, chunks_per_seq)
        is_start = seq_chunk == 0
        ubuf[:, 0:POOL_HEAD, :] = jnp.where(is_start, 0.0, prev[0][:, CHUNK:CHUNK + POOL_HEAD, :])
        xbuf[:, 0:CONV_HEAD, :] = jnp.where(is_start, 0.0, prev[2][:, CHUNK:CHUNK + CONV_HEAD, :])

        def conv_silu(j):
            cols = slice(j * LANES, (j + 1) * LANES)
            acc = convb_ref[:, cols] + xbuf[j, CONV_HEAD:CONV_HEAD + CHUNK, :] * convw_ref[CONV_WIDTH - 1:CONV_WIDTH, cols]
            for k in range(CONV_WIDTH - 1):
                off = CONV_HEAD - (CONV_WIDTH - 1) + k
                acc = acc + xbuf[j, off:off + CHUNK, :] * convw_ref[k:k + 1, cols]
            return _silu(acc)

        pos = seq_chunk * CHUNK + row_i + 1
        for gi, w in enumerate(POOL_WINDOWS):
            cur_u = ubuf[gi, POOL_HEAD:POOL_HEAD + CHUNK, :]
            win_sum = cur_u
            for j in range(1, w):
                win_sum = win_sum + ubuf[gi, POOL_HEAD - j:POOL_HEAD - j + CHUNK, :]
            cnt = jnp.minimum(pos, w).astype(F32)
            diff = win_sum / cnt - cur_u
            yp = _dot(diff.astype(BF16), wpool_ref[gi]) * pscale_ref[:, gi * LANES:(gi + 1) * LANES]
            mixed[mrows, gi * LANES:(gi + 1) * LANES] = yp.astype(BF16)

        dt_all = dtbuf[...]
        a = dt_all * a_neg2
        a_hi = a.astype(BF16)
        a_lo = (a - a_hi.astype(F32)).astype(BF16)
        acs = _dot(tri_ref[...], jnp.concatenate([a_hi, a_lo], axis=0))
        packed = acs + jnp.where(lane >= DT_DUP_LANE, dt_all, 0.0)
        packed_t = packed.T
        acs_t = packed_t[0:DT_DUP_LANE]
        dt_t = packed_t[DT_DUP_LANE:2 * DT_DUP_LANE]
        w_t = jnp.exp2(acs_t[:, CHUNK - 1:CHUNK] - acs_t) * dt_t
        src_t = acs_t - jnp.log2(dt_t)

        for g in range(N_BC_GROUPS):
            b_g = conv_silu(x_slabs + g)
            c_g = conv_silu(x_slabs + N_BC_GROUPS + g)
            b_bf = b_g.astype(BF16)
            c_bf = c_g.astype(BF16)
            b_t = b_g.T
            cb = lax.dot_general(c_bf, b_bf, (((1,), (1,)), ((), ())), preferred_element_type=F32)
            gcol = g * group_width
            st_g = jnp.where(is_start, 0.0, state[:, gcol:gcol + group_width])
            y_off = _dot(c_bf, st_g.astype(BF16))
            fill.step()
            v_slabs = []
            ssq = jnp.zeros((CHUNK, 1), F32)
            for q in range(slabs_per_group):
                col0 = gcol + q * LANES
                h0 = g * heads_per_group + q * HEADS_PER_SLAB
                xq = conv_silu(g * slabs_per_group + q)
                xq_bf = xq.astype(BF16)
                zero = jnp.zeros_like(xq_bf)
                rhs = jnp.concatenate([jnp.where(lo_half, xq_bf, zero), jnp.where(lo_half, zero, xq_bf)], axis=0)
                m_parts, bw_parts, e_parts = [], [], []
                for h in (h0, h0 + 1):
                    bc = jnp.broadcast_to(acs[:, h:h + 1], (CHUNK, LANES))
                    dec_dt = jnp.exp2(jnp.where(causal, bc - src_t[h:h + 1, :], -jnp.inf))
                    m_parts.append((cb * dec_dt).astype(BF16))
                    bw_parts.append((b_t * w_t[h:h + 1, :]).astype(BF16))
                    e_parts.append(jnp.exp2(bc))
                lhs = jnp.concatenate([jnp.concatenate(m_parts, axis=1), jnp.concatenate(bw_parts, axis=1)], axis=0)
                res = _dot(lhs, rhs)
                e_pair = jnp.where(lo_half, e_parts[0], e_parts[1])
                y = res[0:CHUNK] + y_off[:, q * LANES:(q + 1) * LANES] * e_pair + dskip_ref[:, col0:col0 + LANES] * xq
                state[:, col0:col0 + LANES] = (st_g[:, q * LANES:(q + 1) * LANES] * e_pair[CHUNK - 1:CHUNK, :]
                                               + res[CHUNK:2 * CHUNK])
                v = y * _silu(zbuf[:, col0:col0 + LANES])
                ssq = ssq + jnp.sum(v * v, axis=-1, keepdims=True)
                v_slabs.append(v)
                fill.step()
            scale = lax.rsqrt(ssq * (1.0 / group_width) + RMS_EPS)
            for q in range(slabs_per_group):
                col0 = gcol + q * LANES
                vn = v_slabs[q] * scale * normw_ref[:, col0:col0 + LANES]
                mixed[mrows, d_pool + col0:d_pool + col0 + LANES] = vn.astype(BF16)

    points_per_chunk = N_BC_GROUPS * (1 + slabs_per_group)
    first = CHUNKS_PER_STEP * t
    for half in range(2):
        r0 = half * half_rows
        new_sets = [sets[half * CHUNKS_PER_HALF + i] for i in range(CHUNKS_PER_HALF)]
        out_mixed = mixed_refs[half]
        fill = _Interleaver(out_project_tasks(r0, out_mixed) + project_tasks(r0, new_sets),
                            CHUNKS_PER_HALF * points_per_chunk)
        for i in range(CHUNKS_PER_HALF):
            local = (half * CHUNKS_PER_HALF - CHUNKS_PER_HALF + i) % CHUNKS_PER_STEP
            chunk = first + half * CHUNKS_PER_HALF - CHUNKS_PER_HALF + i
            mix_chunk(chunk, sets[local], sets[(local - 1) % CHUNKS_PER_STEP],
                      mixed_refs[local // CHUNKS_PER_HALF], (local % CHUNKS_PER_HALF) * CHUNK, fill)
        fill.drain()


def _ffn_kernel(h_ref, w1_ref, w2_ref, g_ref, b_ref, o_ref, *, alpha, ff_block):
    h = h_ref[...]
    hb = h.astype(BF16)
    d_ff = w1_ref.shape[1]
    acc = alpha * h
    for j in range(d_ff // ff_block):
        hid = _dot(hb, w1_ref[:, j * ff_block:(j + 1) * ff_block])
        act = jnp.square(jnp.maximum(hid, 0.0)).astype(BF16)
        acc = acc + _dot(act, w2_ref[j * ff_block:(j + 1) * ff_block, :])
    o_ref[...] = _layer_norm(acc, g_ref[...], b_ref[...])


def _resident(shape):
    nd = len(shape)
    return pl.BlockSpec(shape, lambda *_: (0,) * nd, pipeline_mode=pl.Buffered(1))


def _mixer_layer(x, w_in, w_pool, pool_scale, conv_w, conv_b, dt_bias, a_log, d_skip, ssd_norm_w,
                 w_out, ln_g, ln_b, *, alpha):
    bsz, seq, d_model = x.shape
    n_tok = bsz * seq
    step_rows = CHUNKS_PER_STEP * CHUNK
    half_rows = CHUNKS_PER_HALF * CHUNK
    n_blocks = n_tok // step_rows
    d_pool = w_pool.shape[0] * w_pool.shape[1]
    n_heads = dt_bias.shape[0]
    d_ssd = n_heads * SSD_HEAD_DIM
    d_conv = conv_w.shape[1]
    d_main = d_pool + d_ssd + d_conv
    d_mix = d_pool + d_ssd
    assert w_in.shape[1] == d_main + n_heads and d_conv == d_ssd + 2 * N_BC_GROUPS * D_STATE
    assert n_heads <= DT_DUP_LANE and DT_DUP_LANE + n_heads <= LANES
    assert n_tok % step_rows == 0 and seq % CHUNK == 0 and w_pool.shape[1] == LANES

    x2d = x.reshape(n_tok, d_model)
    w_main = w_in.astype(BF16)
    w_dt = w_in[:, d_main:]
    w_dt_pad = jnp.zeros((d_model, LANES), F32)
    w_dt_pad = w_dt_pad.at[:, :n_heads].set(w_dt).at[:, DT_DUP_LANE:DT_DUP_LANE + n_heads].set(w_dt).astype(BF16)
    dtb_pad = jnp.zeros((1, LANES), F32).at[0, :n_heads].set(dt_bias).at[0, DT_DUP_LANE:DT_DUP_LANE + n_heads].set(dt_bias)
    alog_pad = jnp.zeros((1, LANES), F32).at[0, :n_heads].set(a_log)
    dskip_row = jnp.repeat(d_skip, SSD_HEAD_DIM)[None, :]
    tri = (jnp.arange(CHUNK)[:, None] >= (jnp.arange(2 * CHUNK)[None, :] % CHUNK)).astype(BF16)

    kern = functools.partial(_mixer_kernel, chunks_per_seq=seq // CHUNK, d_pool=d_pool, d_ssd=d_ssd,
                             n_heads=n_heads, alpha=alpha)
    row = lambda v: v[None, :]

    def chunk_set():
        return [
            pltpu.VMEM((d_pool // LANES, POOL_HEAD + CHUNK, LANES), F32),
            pltpu.VMEM((CHUNK, d_ssd), F32),
            pltpu.VMEM((d_conv // LANES, CONV_HEAD + CHUNK, LANES), F32),
            pltpu.VMEM((CHUNK, LANES), F32),
        ]

    scratch = []
    for _ in range(CHUNKS_PER_STEP):
        scratch += chunk_set()
    scratch += [
        pltpu.VMEM((half_rows, d_mix), BF16),
        pltpu.VMEM((half_rows, d_mix), BF16),
        pltpu.VMEM((D_STATE, d_ssd), F32),
        pltpu.VMEM((half_rows, d_model), BF16),
        pltpu.VMEM((half_rows, d_model), F32),
    ]

    return pl.pallas_call(
        kern,
        grid=(n_blocks + 1,),
        in_specs=[
            pl.BlockSpec((step_rows, d_model), lambda t: (jnp.minimum(t, n_blocks - 1), 0)),
            pl.BlockSpec((step_rows, d_model), lambda t: (jnp.maximum(t - 1, 0), 0)),
            _resident(w_main.shape),
            _resident((d_model, LANES)),
            _resident((CONV_WIDTH, d_conv)),
            _resident((1, d_conv)),
            _resident((1, LANES)),
            _resident((1, LANES)),
            _resident((1, d_ssd)),
            _resident((1, d_ssd)),
            _resident(w_pool.shape),
            _resident((1, d_pool)),
            _resident((d_mix, d_model)),
            _resident((1, d_model)),
            _resident((1, d_model)),
            _resident((CHUNK, 2 * CHUNK)),
        ],
        out_specs=pl.BlockSpec((step_rows, d_model), lambda t: (jnp.maximum(t - 1, 0), 0)),
        out_shape=jax.ShapeDtypeStruct((n_tok, d_model), F32),
        scratch_shapes=scratch,
        compiler_params=pltpu.CompilerParams(
            dimension_semantics=("arbitrary",), vmem_limit_bytes=VMEM_LIMIT_BYTES),
        name="mixer",
    )(x2d, x2d, w_main, w_dt_pad, conv_w, row(conv_b), dtb_pad, alog_pad, dskip_row, row(ssd_norm_w),
      w_pool.astype(BF16), row(pool_scale), w_out.astype(BF16), row(ln_g), row(ln_b), tri)


def _ffn_layer(h, w1, w2, ln_g, ln_b, *, alpha, tm, ff_block):
    n_tok, d_model = h.shape
    d_ff = w1.shape[1]
    assert n_tok % tm == 0 and d_ff % ff_block == 0
    kern = functools.partial(_ffn_kernel, alpha=alpha, ff_block=ff_block)
    return pl.pallas_call(
        kern,
        grid=(n_tok // tm,),
        in_specs=[
            pl.BlockSpec((tm, d_model), lambda i: (i, 0)),
            _resident((d_model, d_ff)),
            _resident((d_ff, d_model)),
            _resident((1, d_model)),
            _resident((1, d_model)),
        ],
        out_specs=pl.BlockSpec((tm, d_model), lambda i: (i, 0)),
        out_shape=jax.ShapeDtypeStruct(h.shape, F32),
        compiler_params=pltpu.CompilerParams(
            dimension_semantics=("arbitrary",), vmem_limit_bytes=VMEM_LIMIT_BYTES),
        name="ffn",
    )(h, w1.astype(BF16), w2.astype(BF16), ln_g[None, :], ln_b[None, :])


def kernel(x, w_in, w_pool, pool_scale, conv_w, conv_b, dt_bias, a_log, d_skip, ssd_norm_w, w_out,
           ln1_g, ln1_b, w_ff1, w_ff2, ln2_g, ln2_b):
    depth = w_in.shape[0]
    alpha = (2.0 * depth) ** 0.25
    bsz, seq, d_model = x.shape
    h = x
    for l in range(depth):
        h = _mixer_layer(h.reshape(bsz, seq, d_model), w_in[l], w_pool[l], pool_scale[l], conv_w[l], conv_b[l],
                         dt_bias[l], a_log[l], d_skip[l], ssd_norm_w[l], w_out[l], ln1_g[l], ln1_b[l], alpha=alpha)
        h = _ffn_layer(h, w_ff1[l], w_ff2[l], ln2_g[l], ln2_b[l], alpha=alpha, tm=FFN_ROWS, ff_block=FFN_COLS)
    return h.reshape(bsz, seq, d_model)
```

```python
import functools
import math

import jax
import jax.numpy as jnp
from jax import lax
from jax.experimental import pallas as pl
from jax.experimental.pallas import tpu as pltpu

POOL_WINDOWS = (2, 4, 8, 16)
N_POOL_GROUPS = len(POOL_WINDOWS)
SSD_HEAD_DIM = 64
N_BC_GROUPS = 4
D_STATE = 128
CONV_WIDTH = 4
CHUNK = 128
LN_EPS = 1e-5
RMS_EPS = 1e-5

LANES = 128
MXU_COLS = 256
CONV_HEAD = 8
POOL_HEAD = 16
DT_DUP_LANE = 32
HEADS_PER_SLAB = LANES // SSD_HEAD_DIM
VMEM_LIMIT_BYTES = 56 * 1024 * 1024
CHUNKS_PER_HALF = 1
CHUNKS_PER_STEP = 2 * CHUNKS_PER_HALF
FFN_K_PARTS = 2
LOG2E = math.log2(math.e)

BF16 = jnp.bfloat16
F32 = jnp.float32


def _silu(v):
    return v * (1.0 / (1.0 + jnp.exp2(v * (-LOG2E))))


def _softplus(v):
    return jnp.maximum(v, 0.0) + jnp.log1p(jnp.exp(-jnp.abs(v)))


def _layer_norm(v, g, b):
    mu = jnp.mean(v, axis=-1, keepdims=True)
    d = v - mu
    var = jnp.mean(d * d, axis=-1, keepdims=True)
    return d * lax.rsqrt(var + LN_EPS) * g + b


def _dot(a, b):
    return jnp.dot(a, b, preferred_element_type=F32)


class _Interleaver:
    def __init__(self, tasks, points):
        self._tasks = list(tasks)
        self._total = len(self._tasks)
        self._points = points
        self._seen = 0
        self._done = 0

    def step(self):
        left = self._points - self._seen
        self._seen += 1
        target = self._done + -(-(self._total - self._done) // max(left, 1))
        while self._done < min(target, self._total):
            self._tasks[self._done]()
            self._done += 1

    def drain(self):
        while self._done < self._total:
            self._tasks[self._done]()
            self._done += 1


def _block_kernel(x_ref, xprev_ref, wmain_ref, wdt_ref, convw_ref, convb_ref, dtb_ref, alog_ref, dskip_ref,
                  normw_ref, wpool_ref, pscale_ref, wout_ref, g_ref, b_ref, tri_ref, w1_ref, w2_ref, g2_ref, b2_ref,
                  o_ref,
                  *scratch,
                  chunks_per_seq, d_pool, d_ssd, n_heads, alpha):
    t = pl.program_id(0)
    heads_per_group = n_heads // N_BC_GROUPS
    slabs_per_group = heads_per_group // HEADS_PER_SLAB
    group_width = d_ssd // N_BC_GROUPS
    x_slabs = d_ssd // LANES
    half_rows = CHUNKS_PER_HALF * CHUNK
    sets = [scratch[4 * i:4 * i + 4] for i in range(CHUNKS_PER_STEP)]
    mixed_refs = scratch[4 * CHUNKS_PER_STEP:4 * CHUNKS_PER_STEP + 2]
    state, xb_ref, hpre_ref, hf0, hf1, hb0, hb1, act_ref, facc_ref = scratch[4 * CHUNKS_PER_STEP + 2:]
    hf_refs, hb_refs = (hf0, hf1), (hb0, hb1)

    @pl.when(t == 0)
    def _():
        for ref in scratch:
            ref[...] = jnp.zeros(ref.shape, ref.dtype)

    lane = lax.broadcasted_iota(jnp.int32, (1, LANES), 1)
    row_i = lax.broadcasted_iota(jnp.int32, (CHUNK, LANES), 0)
    col_i = lax.broadcasted_iota(jnp.int32, (CHUNK, LANES), 1)
    causal = row_i >= col_i
    lo_half = lane < SSD_HEAD_DIM
    a_neg2 = jnp.where(lane < n_heads, -LOG2E * jnp.exp(alog_ref[...]), 0.0)

    def project_tasks(r0, dsts):
        slabs_per_dot = MXU_COLS // LANES

        def cast():
            xb_ref[...] = x_ref[r0:r0 + half_rows, :].astype(BF16)

        def slab_task(which, col0, j0, head):
            def run():
                res = _dot(xb_ref[...], wmain_ref[:, col0 + j0 * LANES:col0 + (j0 + slabs_per_dot) * LANES])
                for ci, dst in enumerate(dsts):
                    for dj in range(slabs_per_dot):
                        dst[which][j0 + dj, head:head + CHUNK, :] = res[ci * CHUNK:(ci + 1) * CHUNK, dj * LANES:(dj + 1) * LANES]
            return run

        def gate_task(c0):
            def run():
                res = _dot(xb_ref[...], wmain_ref[:, d_pool + c0:d_pool + c0 + MXU_COLS])
                for ci, dst in enumerate(dsts):
                    dst[1][:, c0:c0 + MXU_COLS] = res[ci * CHUNK:(ci + 1) * CHUNK]
            return run

        def dt_task():
            res = _softplus(_dot(xb_ref[...], wdt_ref[...]) + dtb_ref[...])
            for ci, dst in enumerate(dsts):
                dst[3][...] = res[ci * CHUNK:(ci + 1) * CHUNK]

        n_conv_slabs, n_pool_slabs = dsts[0][2].shape[0], dsts[0][0].shape[0]
        tasks = [cast, dt_task]
        tasks += [slab_task(2, d_pool + d_ssd, j0, CONV_HEAD) for j0 in range(0, n_conv_slabs, slabs_per_dot)]
        tasks += [slab_task(0, 0, j0, POOL_HEAD) for j0 in range(0, n_pool_slabs, slabs_per_dot)]
        tasks += [gate_task(c0) for c0 in range(0, d_ssd, MXU_COLS)]
        return tasks

    def out_project_tasks(r0, mixed, half):
        d_model = wout_ref.shape[1]

        def block_task(c0):
            def run():
                hpre_ref[:, c0:c0 + MXU_COLS] = _dot(mixed[...], wout_ref[:, c0:c0 + MXU_COLS])
            return run

        def norm_task(ci):
            def run():
                rows = slice(ci * CHUNK, (ci + 1) * CHUNK)
                pre = alpha * xprev_ref[r0 + ci * CHUNK:r0 + (ci + 1) * CHUNK, :] + hpre_ref[rows, :]
                hn = _layer_norm(pre, g_ref[...], b_ref[...])
                hf_refs[half][rows, :] = hn
                hb_refs[half][rows, :] = hn.astype(BF16)
            return run

        return [block_task(c0) for c0 in range(0, d_model, MXU_COLS)] + [norm_task(ci) for ci in range(CHUNKS_PER_HALF)]

    def ffn_tasks(r0, half):
        d_model, d_ff = w1_ref.shape
        k_part = d_ff // FFN_K_PARTS

        def up_task(c0):
            def run():
                hid = _dot(hb_refs[half][...], w1_ref[:, c0:c0 + MXU_COLS])
                act_ref[:, c0:c0 + MXU_COLS] = jnp.square(jnp.maximum(hid, 0.0)).astype(BF16)
            return run

        def down_task(kp, c0):
            def run():
                part = _dot(act_ref[:, kp * k_part:(kp + 1) * k_part], w2_ref[kp * k_part:(kp + 1) * k_part, c0:c0 + MXU_COLS])
                if kp == 0:
                    facc_ref[:, c0:c0 + MXU_COLS] = part
                else:
                    facc_ref[:, c0:c0 + MXU_COLS] += part
            return run

        def norm_task(ci):
            def run():
                rows = slice(ci * CHUNK, (ci + 1) * CHUNK)
                pre = alpha * hf_refs[half][rows, :] + facc_ref[rows, :]
                o_ref[r0 + ci * CHUNK:r0 + (ci + 1) * CHUNK, :] = _layer_norm(pre, g2_ref[...], b2_ref[...])
            return run

        tasks = []
        for kp in range(FFN_K_PARTS):
            tasks += [up_task(c0) for c0 in range(kp * k_part, (kp + 1) * k_part, MXU_COLS)]
            tasks += [down_task(kp, c0) for c0 in range(0, d_model, MXU_COLS)]
        return tasks + [norm_task(ci) for ci in range(CHUNKS_PER_HALF)]

    def mix_chunk(chunk, cur, prev, mixed, m0, fill):
        ubuf, zbuf, xbuf, dtbuf = cur
        mrows = slice(m0, m0 + CHUNK)
        seq_chunk = lax.rem(chunk + chunks_per_seq, chunks_per_seq)
        is_start = seq_chunk == 0
        ubuf[:, 0:POOL_HEAD, :] = jnp.where(is_start, 0.0, prev[0][:, CHUNK:CHUNK + POOL_HEAD, :])
        xbuf[:, 0:CONV_HEAD, :] = jnp.where(is_start, 0.0, prev[2][:, CHUNK:CHUNK + CONV_HEAD, :])

        def conv_silu(j):
            cols = slice(j * LANES, (j + 1) * LANES)
            acc = convb_ref[:, cols] + xbuf[j, CONV_HEAD:CONV_HEAD + CHUNK, :] * convw_ref[CONV_WIDTH - 1:CONV_WIDTH, cols]
            for k in range(CONV_WIDTH - 1):
                off = CONV_HEAD - (CONV_WIDTH - 1) + k
                acc = acc + xbuf[j, off:off + CHUNK, :] * convw_ref[k:k + 1, cols]
            return _silu(acc)

        pos = seq_chunk * CHUNK + row_i + 1
        for gi, w in enumerate(POOL_WINDOWS):
            cur_u = ubuf[gi, POOL_HEAD:POOL_HEAD + CHUNK, :]
            win_sum = cur_u
            for j in range(1, w):
                win_sum = win_sum + ubuf[gi, POOL_HEAD - j:POOL_HEAD - j + CHUNK, :]
            cnt = jnp.minimum(pos, w).astype(F32)
            diff = win_sum / cnt - cur_u
            yp = _dot(diff.astype(BF16), wpool_ref[gi]) * pscale_ref[:, gi * LANES:(gi + 1) * LANES]
            mixed[mrows, gi * LANES:(gi + 1) * LANES] = yp.astype(BF16)

        dt_all = dtbuf[...]
        a = dt_all * a_neg2
        a_hi = a.astype(BF16)
        a_lo = (a - a_hi.astype(F32)).astype(BF16)
        acs = _dot(tri_ref[...], jnp.concatenate([a_hi, a_lo], axis=0))
        packed = acs + jnp.where(lane >= DT_DUP_LANE, dt_all, 0.0)
        packed_t = packed.T
        acs_t = packed_t[0:DT_DUP_LANE]
        dt_t = packed_t[DT_DUP_LANE:2 * DT_DUP_LANE]
        w_t = jnp.exp2(acs_t[:, CHUNK - 1:CHUNK] - acs_t) * dt_t
        src_t = acs_t - jnp.log2(dt_t)

        for g in range(N_BC_GROUPS):
            b_g = conv_silu(x_slabs + g)
            c_g = conv_silu(x_slabs + N_BC_GROUPS + g)
            b_bf = b_g.astype(BF16)
            c_bf = c_g.astype(BF16)
            b_t = b_g.T
            cb = lax.dot_general(c_bf, b_bf, (((1,), (1,)), ((), ())), preferred_element_type=F32)
            gcol = g * group_width
            st_g = jnp.where(is_start, 0.0, state[:, gcol:gcol + group_width])
            y_off = _dot(c_bf, st_g.astype(BF16))
            fill.step()
            v_slabs = []
            ssq = jnp.zeros((CHUNK, 1), F32)
            for q in range(slabs_per_group):
                col0 = gcol + q * LANES
                h0 = g * heads_per_group + q * HEADS_PER_SLAB
                xq = conv_silu(g * slabs_per_group + q)
                xq_bf = xq.astype(BF16)
                zero = jnp.zeros_like(xq_bf)
                rhs = jnp.concatenate([jnp.where(lo_half, xq_bf, zero), jnp.where(lo_half, zero, xq_bf)], axis=0)
                m_parts, bw_parts, e_parts = [], [], []
                for h in (h0, h0 + 1):
                    bc = jnp.broadcast_to(acs[:, h:h + 1], (CHUNK, LANES))
                    dec_dt = jnp.exp2(jnp.where(causal, bc - src_t[h:h + 1, :], -jnp.inf))
                    m_parts.append((cb * dec_dt).astype(BF16))
                    bw_parts.append((b_t * w_t[h:h + 1, :]).astype(BF16))
                    e_parts.append(jnp.exp2(bc))
                lhs = jnp.concatenate([jnp.concatenate(m_parts, axis=1), jnp.concatenate(bw_parts, axis=1)], axis=0)
                res = _dot(lhs, rhs)
                e_pair = jnp.where(lo_half, e_parts[0], e_parts[1])
                y = res[0:CHUNK] + y_off[:, q * LANES:(q + 1) * LANES] * e_pair + dskip_ref[:, col0:col0 + LANES] * xq
                state[:, col0:col0 + LANES] = (st_g[:, q * LANES:(q + 1) * LANES] * e_pair[CHUNK - 1:CHUNK, :]
                                               + res[CHUNK:2 * CHUNK])
                v = y * _silu(zbuf[:, col0:col0 + LANES])
                ssq = ssq + jnp.sum(v * v, axis=-1, keepdims=True)
                v_slabs.append(v)
                fill.step()
            scale = lax.rsqrt(ssq * (1.0 / group_width) + RMS_EPS)
            for q in range(slabs_per_group):
                col0 = gcol + q * LANES
                vn = v_slabs[q] * scale * normw_ref[:, col0:col0 + LANES]
                mixed[mrows, d_pool + col0:d_pool + col0 + LANES] = vn.astype(BF16)

    points_per_chunk = N_BC_GROUPS * (1 + slabs_per_group)
    first = CHUNKS_PER_STEP * t
    for half in range(2):
        r0 = half * half_rows
        new_sets = [sets[half * CHUNKS_PER_HALF + i] for i in range(CHUNKS_PER_HALF)]
        tasks = ffn_tasks(r0, half) + out_project_tasks(r0, mixed_refs[half], half) + project_tasks(r0, new_sets)
        fill = _Interleaver(tasks, CHUNKS_PER_HALF * points_per_chunk)
        for i in range(CHUNKS_PER_HALF):
            local = (half * CHUNKS_PER_HALF - CHUNKS_PER_HALF + i) % CHUNKS_PER_STEP
            chunk = first + half * CHUNKS_PER_HALF - CHUNKS_PER_HALF + i
            mix_chunk(chunk, sets[local], sets[(local - 1) % CHUNKS_PER_STEP],
                      mixed_refs[local // CHUNKS_PER_HALF], (local % CHUNKS_PER_HALF) * CHUNK, fill)
        fill.drain()


def _resident(shape):
    nd = len(shape)
    return pl.BlockSpec(shape, lambda *_: (0,) * nd, pipeline_mode=pl.Buffered(1))


def _block_layer(x, w_in, w_pool, pool_scale, conv_w, conv_b, dt_bias, a_log, d_skip, ssd_norm_w,
                 w_out, ln_g, ln_b, w1, w2, ln2_g, ln2_b, *, alpha):
    bsz, seq, d_model = x.shape
    n_tok = bsz * seq
    step_rows = CHUNKS_PER_STEP * CHUNK
    half_rows = CHUNKS_PER_HALF * CHUNK
    n_blocks = n_tok // step_rows
    d_pool = w_pool.shape[0] * w_pool.shape[1]
    n_heads = dt_bias.shape[0]
    d_ssd = n_heads * SSD_HEAD_DIM
    d_conv = conv_w.shape[1]
    d_main = d_pool + d_ssd + d_conv
    d_mix = d_pool + d_ssd
    d_ff = w1.shape[1]
    assert w_in.shape[1] == d_main + n_heads and d_conv == d_ssd + 2 * N_BC_GROUPS * D_STATE
    assert n_heads <= DT_DUP_LANE and DT_DUP_LANE + n_heads <= LANES
    assert n_tok % step_rows == 0 and seq % CHUNK == 0 and w_pool.shape[1] == LANES
    assert d_ff % (FFN_K_PARTS * MXU_COLS) == 0 and d_model % MXU_COLS == 0

    x2d = x.reshape(n_tok, d_model)
    w_main = w_in.astype(BF16)
    w_dt = w_in[:, d_main:]
    w_dt_pad = jnp.zeros((d_model, LANES), F32)
    w_dt_pad = w_dt_pad.at[:, :n_heads].set(w_dt).at[:, DT_DUP_LANE:DT_DUP_LANE + n_heads].set(w_dt).astype(BF16)
    dtb_pad = jnp.zeros((1, LANES), F32).at[0, :n_heads].set(dt_bias).at[0, DT_DUP_LANE:DT_DUP_LANE + n_heads].set(dt_bias)
    alog_pad = jnp.zeros((1, LANES), F32).at[0, :n_heads].set(a_log)
    dskip_row = jnp.repeat(d_skip, SSD_HEAD_DIM)[None, :]
    tri = (jnp.arange(CHUNK)[:, None] >= (jnp.arange(2 * CHUNK)[None, :] % CHUNK)).astype(BF16)

    kern = functools.partial(_block_kernel, chunks_per_seq=seq // CHUNK, d_pool=d_pool, d_ssd=d_ssd,
                             n_heads=n_heads, alpha=alpha)
    row = lambda v: v[None, :]

    def chunk_set():
        return [
            pltpu.VMEM((d_pool // LANES, POOL_HEAD + CHUNK, LANES), F32),
            pltpu.VMEM((CHUNK, d_ssd), F32),
            pltpu.VMEM((d_conv // LANES, CONV_HEAD + CHUNK, LANES), F32),
            pltpu.VMEM((CHUNK, LANES), F32),
        ]

    scratch = []
    for _ in range(CHUNKS_PER_STEP):
        scratch += chunk_set()
    scratch += [
        pltpu.VMEM((half_rows, d_mix), BF16),
        pltpu.VMEM((half_rows, d_mix), BF16),
        pltpu.VMEM((D_STATE, d_ssd), F32),
        pltpu.VMEM((half_rows, d_model), BF16),
        pltpu.VMEM((half_rows, d_model), F32),
        pltpu.VMEM((half_rows, d_model), F32),
        pltpu.VMEM((half_rows, d_model), F32),
        pltpu.VMEM((half_rows, d_model), BF16),
        pltpu.VMEM((half_rows, d_model), BF16),
        pltpu.VMEM((half_rows, d_ff), BF16),
        pltpu.VMEM((half_rows, d_model), F32),
    ]

    out = pl.pallas_call(
        kern,
        grid=(n_blocks + 2,),
        in_specs=[
            pl.BlockSpec((step_rows, d_model), lambda t: (jnp.minimum(t, n_blocks - 1), 0)),
            pl.BlockSpec((step_rows, d_model), lambda t: (jnp.clip(t - 1, 0, n_blocks - 1), 0)),
            _resident(w_main.shape),
            _resident((d_model, LANES)),
            _resident((CONV_WIDTH, d_conv)),
            _resident((1, d_conv)),
            _resident((1, LANES)),
            _resident((1, LANES)),
            _resident((1, d_ssd)),
            _resident((1, d_ssd)),
            _resident(w_pool.shape),
            _resident((1, d_pool)),
            _resident((d_mix, d_model)),
            _resident((1, d_model)),
            _resident((1, d_model)),
            _resident((CHUNK, 2 * CHUNK)),
            _resident(w1.shape),
            _resident(w2.shape),
            _resident((1, d_model)),
            _resident((1, d_model)),
        ],
        out_specs=pl.BlockSpec((step_rows, d_model), lambda t: (jnp.maximum(t - 2, 0), 0)),
        out_shape=jax.ShapeDtypeStruct((n_tok, d_model), F32),
        scratch_shapes=scratch,
        compiler_params=pltpu.CompilerParams(
            dimension_semantics=("arbitrary",), vmem_limit_bytes=VMEM_LIMIT_BYTES),
        name="block",
    )(x2d, x2d, w_main, w_dt_pad, conv_w, row(conv_b), dtb_pad, alog_pad, dskip_row, row(ssd_norm_w),
      w_pool.astype(BF16), row(pool_scale), w_out.astype(BF16), row(ln_g), row(ln_b), tri,
      w1.astype(BF16), w2.astype(BF16), row(ln2_g), row(ln2_b))
    return out.reshape(bsz, seq, d_model)


def kernel(x, w_in, w_pool, pool_scale, conv_w, conv_b, dt_bias, a_log, d_skip, ssd_norm_w, w_out,
           ln1_g, ln1_b, w_ff1, w_ff2, ln2_g, ln2_b):
    depth = w_in.shape[0]
    alpha = (2.0 * depth) ** 0.25
    h = x
    for l in range(depth):
        h = _block_layer(h, w_in[l], w_pool[l], pool_scale[l], conv_w[l], conv_b[l], dt_bias[l], a_log[l],
                         d_skip[l], ssd_norm_w[l], w_out[l], ln1_g[l], ln1_b[l],
                         w_ff1[l], w_ff2[l], ln2_g[l], ln2_b[l], alpha=alpha)
    return h
```

```python
import functools
import math

import jax
import jax.numpy as jnp
from jax import lax
from jax.experimental import pallas as pl
from jax.experimental.pallas import tpu as pltpu

POOL_WINDOWS = (2, 4, 8, 16)
N_POOL_GROUPS = len(POOL_WINDOWS)
SSD_HEAD_DIM = 64
N_BC_GROUPS = 4
D_STATE = 128
CONV_WIDTH = 4
CHUNK = 128
LN_EPS = 1e-5
RMS_EPS = 1e-5

LANES = 128
MXU_COLS = 256
CONV_HEAD = 8
POOL_HEAD = 16
DT_DUP_LANE = 32
HEADS_PER_SLAB = LANES // SSD_HEAD_DIM
VMEM_LIMIT_BYTES = 56 * 1024 * 1024
CHUNKS_PER_HALF = 2
CHUNKS_PER_STEP = 2 * CHUNKS_PER_HALF
FFN_ROWS = 1024
FFN_COLS = 1024
LOG2E = math.log2(math.e)

BF16 = jnp.bfloat16
F32 = jnp.float32


def _silu(v):
    return v * (1.0 / (1.0 + jnp.exp2(v * (-LOG2E))))


def _softplus(v):
    return jnp.maximum(v, 0.0) + jnp.log1p(jnp.exp(-jnp.abs(v)))


def _layer_norm(v, g, b):
    mu = jnp.mean(v, axis=-1, keepdims=True)
    d = v - mu
    var = jnp.mean(d * d, axis=-1, keepdims=True)
    return d * lax.rsqrt(var + LN_EPS) * g + b


def _dot(a, b):
    return jnp.dot(a, b, preferred_element_type=F32)


class _Interleaver:
    def __init__(self, tasks, points):
        self._tasks = list(tasks)
        self._total = len(self._tasks)
        self._points = points
        self._seen = 0
        self._done = 0

    def step(self):
        self._seen += 1
        target = min(self._total, -(-self._seen * self._total // self._points))
        while self._done < target:
            self._tasks[self._done]()
            self._done += 1

    def drain(self):
        while self._done < self._total:
            self._tasks[self._done]()
            self._done += 1


def _mixer_kernel(x_ref, xprev_ref, wmain_ref, wdt_ref, convw_ref, convb_ref, dtb_ref, alog_ref, dskip_ref,
                  normw_ref, wpool_ref, pscale_ref, wout_ref, g_ref, b_ref, tri_ref,
                  h_ref,
                  *scratch,
                  chunks_per_seq, d_pool, d_ssd, n_heads, alpha):
    t = pl.program_id(0)
    heads_per_group = n_heads // N_BC_GROUPS
    slabs_per_group = heads_per_group // HEADS_PER_SLAB
    group_width = d_ssd // N_BC_GROUPS
    x_slabs = d_ssd // LANES
    half_rows = CHUNKS_PER_HALF * CHUNK
    sets = [scratch[4 * i:4 * i + 4] for i in range(CHUNKS_PER_STEP)]
    mixed_refs = scratch[4 * CHUNKS_PER_STEP:4 * CHUNKS_PER_STEP + 2]
    state, xb_ref, hpre_ref = scratch[4 * CHUNKS_PER_STEP + 2:]

    @pl.when(t == 0)
    def _():
        for ref in scratch:
            ref[...] = jnp.zeros(ref.shape, ref.dtype)

    lane = lax.broadcasted_iota(jnp.int32, (1, LANES), 1)
    row_i = lax.broadcasted_iota(jnp.int32, (CHUNK, LANES), 0)
    col_i = lax.broadcasted_iota(jnp.int32, (CHUNK, LANES), 1)
    causal = row_i >= col_i
    lo_half = lane < SSD_HEAD_DIM
    a_neg2 = jnp.where(lane < n_heads, -LOG2E * jnp.exp(alog_ref[...]), 0.0)

    def project_tasks(r0, dsts):
        slabs_per_dot = MXU_COLS // LANES

        def cast():
            xb_ref[...] = x_ref[r0:r0 + half_rows, :].astype(BF16)

        def slab_task(which, col0, j0, head):
            def run():
                res = _dot(xb_ref[...], wmain_ref[:, col0 + j0 * LANES:col0 + (j0 + slabs_per_dot) * LANES])
                for ci, dst in enumerate(dsts):
                    for dj in range(slabs_per_dot):
                        dst[which][j0 + dj, head:head + CHUNK, :] = res[ci * CHUNK:(ci + 1) * CHUNK, dj * LANES:(dj + 1) * LANES]
            return run

        def gate_task(c0):
            def run():
                res = _silu(_dot(xb_ref[...], wmain_ref[:, d_pool + c0:d_pool + c0 + MXU_COLS]))
                for ci, dst in enumerate(dsts):
                    dst[1][:, c0:c0 + MXU_COLS] = res[ci * CHUNK:(ci + 1) * CHUNK]
            return run

        def dt_task():
            res = _softplus(_dot(xb_ref[...], wdt_ref[...]) + dtb_ref[...])
            for ci, dst in enumerate(dsts):
                dst[3][...] = res[ci * CHUNK:(ci + 1) * CHUNK]

        n_conv_slabs, n_pool_slabs = dsts[0][2].shape[0], dsts[0][0].shape[0]
        tasks = [cast, dt_task]
        tasks += [slab_task(2, d_pool + d_ssd, j0, CONV_HEAD) for j0 in range(0, n_conv_slabs, slabs_per_dot)]
        tasks += [slab_task(0, 0, j0, POOL_HEAD) for j0 in range(0, n_pool_slabs, slabs_per_dot)]
        tasks += [gate_task(c0) for c0 in range(0, d_ssd, MXU_COLS)]
        return tasks

    def out_project_tasks(r0, mixed):
        d_model = wout_ref.shape[1]

        def block_task(c0):
            def run():
                hpre_ref[:, c0:c0 + MXU_COLS] = _dot(mixed[...], wout_ref[:, c0:c0 + MXU_COLS])
            return run

        def norm_task(ci):
            def run():
                rows = slice(ci * CHUNK, (ci + 1) * CHUNK)
                pre = alpha * xprev_ref[r0 + ci * CHUNK:r0 + (ci + 1) * CHUNK, :] + hpre_ref[rows, :]
                h_ref[r0 + ci * CHUNK:r0 + (ci + 1) * CHUNK, :] = _layer_norm(pre, g_ref[...], b_ref[...])
            return run

        return [block_task(c0) for c0 in range(0, d_model, MXU_COLS)] + [norm_task(ci) for ci in range(CHUNKS_PER_HALF)]

    def mix_chunk(chunk, cur, prev, mixed, m0, fill):
        ubuf, zbuf, xbuf, dtbuf = cur
        mrows = slice(m0, m0 + CHUNK)
        seq_chunk = lax.rem(chunk + chunks_per_seq, chunks_per_seq)
        is_start = seq_chunk == 0
        ubuf[:, 0:POOL_HEAD, :] = jnp.where(is_start, 0.0, prev[0][:, CHUNK:CHUNK + POOL_HEAD, :])
        xbuf[:, 0:CONV_HEAD, :] = jnp.where(is_start, 0.0, prev[2][:, CHUNK:CHUNK + CONV_HEAD, :])

        def conv_silu(j):
            cols = slice(j * LANES, (j + 1) * LANES)
            acc = convb_ref[:, cols] + xbuf[j, CONV_HEAD:CONV_HEAD + CHUNK, :] * convw_ref[CONV_WIDTH - 1:CONV_WIDTH, cols]
            for k in range(CONV_WIDTH - 1):
                off = CONV_HEAD - (CONV_WIDTH - 1) + k
                acc = acc + xbuf[j, off:off + CHUNK, :] * convw_ref[k:k + 1, cols]
            return _silu(acc)

        pos = seq_chunk * CHUNK + row_i + 1
        for gi, w in enumerate(POOL_WINDOWS):
            cur_u = ubuf[gi, POOL_HEAD:POOL_HEAD + CHUNK, :]
            win_sum = cur_u
            for j in range(1, w):
                win_sum = win_sum + ubuf[gi, POOL_HEAD - j:POOL_HEAD - j + CHUNK, :]
            cnt = jnp.minimum(pos, w).astype(F32)
            diff = win_sum / cnt - cur_u
            yp = _dot(diff.astype(BF16), wpool_ref[gi]) * pscale_ref[:, gi * LANES:(gi + 1) * LANES]
            mixed[mrows, gi * LANES:(gi + 1) * LANES] = yp.astype(BF16)

        dt_all = dtbuf[...]
        a = dt_all * a_neg2
        a_hi = a.astype(BF16)
        a_lo = (a - a_hi.astype(F32)).astype(BF16)
        acs = _dot(tri_ref[...], jnp.concatenate([a_hi, a_lo], axis=0))
        packed = acs + jnp.where(lane >= DT_DUP_LANE, dt_all, 0.0)
        packed_t = packed.T
        acs_t = packed_t[0:DT_DUP_LANE]
        dt_t = packed_t[DT_DUP_LANE:2 * DT_DUP_LANE]
        w_t = jnp.exp2(acs_t[:, CHUNK - 1:CHUNK] - acs_t) * dt_t
        src_t = acs_t - jnp.log2(dt_t)

        for g in range(N_BC_GROUPS):
            b_g = conv_silu(x_slabs + g)
            c_g = conv_silu(x_slabs + N_BC_GROUPS + g)
            b_bf = b_g.astype(BF16)
            c_bf = c_g.astype(BF16)
            b_t = b_g.T
            cb = lax.dot_general(c_bf, b_bf, (((1,), (1,)), ((), ())), preferred_element_type=F32)
            gcol = g * group_width
            st_g = jnp.where(is_start, 0.0, state[:, gcol:gcol + group_width])
            y_off = _dot(c_bf, st_g.astype(BF16))
            fill.step()
            v_slabs = []
            ssq = jnp.zeros((CHUNK, 1), F32)
            for q in range(slabs_per_group):
                col0 = gcol + q * LANES
                h0 = g * heads_per_group + q * HEADS_PER_SLAB
                xq = conv_silu(g * slabs_per_group + q)
                xq_bf = xq.astype(BF16)
                zero = jnp.zeros_like(xq_bf)
                rhs = jnp.concatenate([jnp.where(lo_half, xq_bf, zero), jnp.where(lo_half, zero, xq_bf)], axis=0)
                m_parts, bw_parts, e_parts = [], [], []
                for h in (h0, h0 + 1):
                    bc = jnp.broadcast_to(acs[:, h:h + 1], (CHUNK, LANES))
                    dec_dt = jnp.exp2(jnp.where(causal, bc - src_t[h:h + 1, :], -jnp.inf))
                    m_parts.append((cb * dec_dt).astype(BF16))
                    bw_parts.append((b_t * w_t[h:h + 1, :]).astype(BF16))
                    e_parts.append(jnp.exp2(bc))
                lhs = jnp.concatenate([jnp.concatenate(m_parts, axis=1), jnp.concatenate(bw_parts, axis=1)], axis=0)
                res = _dot(lhs, rhs)
                e_pair = jnp.where(lo_half, e_parts[0], e_parts[1])
                y = res[0:CHUNK] + y_off[:, q * LANES:(q + 1) * LANES] * e_pair + dskip_ref[:, col0:col0 + LANES] * xq
                state[:, col0:col0 + LANES] = (st_g[:, q * LANES:(q + 1) * LANES] * e_pair[CHUNK - 1:CHUNK, :]
                                               + res[CHUNK:2 * CHUNK])
                v = y * zbuf[:, col0:col0 + LANES]
                ssq = ssq + jnp.sum(v * v, axis=-1, keepdims=True)
                v_slabs.append(v)
                fill.step()
            scale = lax.rsqrt(ssq * (1.0 / group_width) + RMS_EPS)
            for q in range(slabs_per_group):
                col0 = gcol + q * LANES
                vn = v_slabs[q] * scale * normw_ref[:, col0:col0 + LANES]
                mixed[mrows, d_pool + col0:d_pool + col0 + LANES] = vn.astype(BF16)

    points_per_chunk = N_BC_GROUPS * (1 + slabs_per_group)
    first = CHUNKS_PER_STEP * t
    for half in range(2):
        r0 = half * half_rows
        new_sets = [sets[half * CHUNKS_PER_HALF + i] for i in range(CHUNKS_PER_HALF)]
        tasks = out_project_tasks(r0, mixed_refs[half]) + project_tasks(r0, new_sets)
        fill = _Interleaver(tasks, CHUNKS_PER_HALF * points_per_chunk)
        for i in range(CHUNKS_PER_HALF):
            local = (half * CHUNKS_PER_HALF - CHUNKS_PER_HALF + i) % CHUNKS_PER_STEP
            chunk = first + half * CHUNKS_PER_HALF - CHUNKS_PER_HALF + i
            mix_chunk(chunk, sets[local], sets[(local - 1) % CHUNKS_PER_STEP],
                      mixed_refs[local // CHUNKS_PER_HALF], (local % CHUNKS_PER_HALF) * CHUNK, fill)
        fill.drain()


def _ffn_kernel(h_ref, w1_ref, w2_ref, g_ref, b_ref, o_ref, *, alpha, ff_block):
    h = h_ref[...]
    hb = h.astype(BF16)
    d_ff = w1_ref.shape[1]
    acc = alpha * h
    for j in range(d_ff // ff_block):
        hid = _dot(hb, w1_ref[:, j * ff_block:(j + 1) * ff_block])
        act = jnp.square(jnp.maximum(hid, 0.0)).astype(BF16)
        acc = acc + _dot(act, w2_ref[j * ff_block:(j + 1) * ff_block, :])
    o_ref[...] = _layer_norm(acc, g_ref[...], b_ref[...])


def _resident(shape):
    nd = len(shape)
    return pl.BlockSpec(shape, lambda *_: (0,) * nd, pipeline_mode=pl.Buffered(1))


def _mixer_layer(x, w_in, w_pool, pool_scale, conv_w, conv_b, dt_bias, a_log, d_skip, ssd_norm_w,
                 w_out, ln_g, ln_b, *, alpha):
    bsz, seq, d_model = x.shape
    n_tok = bsz * seq
    step_rows = CHUNKS_PER_STEP * CHUNK
    half_rows = CHUNKS_PER_HALF * CHUNK
    n_blocks = n_tok // step_rows
    d_pool = w_pool.shape[0] * w_pool.shape[1]
    n_heads = dt_bias.shape[0]
    d_ssd = n_heads * SSD_HEAD_DIM
    d_conv = conv_w.shape[1]
    d_main = d_pool + d_ssd + d_conv
    d_mix = d_pool + d_ssd
    assert w_in.shape[1] == d_main + n_heads and d_conv == d_ssd + 2 * N_BC_GROUPS * D_STATE
    assert n_heads <= DT_DUP_LANE and DT_DUP_LANE + n_heads <= LANES
    assert n_tok % step_rows == 0 and seq % CHUNK == 0 and w_pool.shape[1] == LANES

    x2d = x.reshape(n_tok, d_model)
    w_main = w_in.astype(BF16)
    w_dt = w_in[:, d_main:]
    w_dt_pad = jnp.zeros((d_model, LANES), F32)
    w_dt_pad = w_dt_pad.at[:, :n_heads].set(w_dt).at[:, DT_DUP_LANE:DT_DUP_LANE + n_heads].set(w_dt).astype(BF16)
    dtb_pad = jnp.zeros((1, LANES), F32).at[0, :n_heads].set(dt_bias).at[0, DT_DUP_LANE:DT_DUP_LANE + n_heads].set(dt_bias)
    alog_pad = jnp.zeros((1, LANES), F32).at[0, :n_heads].set(a_log)
    dskip_row = jnp.repeat(d_skip, SSD_HEAD_DIM)[None, :]
    tri = (jnp.arange(CHUNK)[:, None] >= (jnp.arange(2 * CHUNK)[None, :] % CHUNK)).astype(BF16)

    kern = functools.partial(_mixer_kernel, chunks_per_seq=seq // CHUNK, d_pool=d_pool, d_ssd=d_ssd,
                             n_heads=n_heads, alpha=alpha)
    row = lambda v: v[None, :]

    def chunk_set():
        return [
            pltpu.VMEM((d_pool // LANES, POOL_HEAD + CHUNK, LANES), F32),
            pltpu.VMEM((CHUNK, d_ssd), F32),
            pltpu.VMEM((d_conv // LANES, CONV_HEAD + CHUNK, LANES), F32),
            pltpu.VMEM((CHUNK, LANES), F32),
        ]

    scratch = []
    for _ in range(CHUNKS_PER_STEP):
        scratch += chunk_set()
    scratch += [
        pltpu.VMEM((half_rows, d_mix), BF16),
        pltpu.VMEM((half_rows, d_mix), BF16),
        pltpu.VMEM((D_STATE, d_ssd), F32),
        pltpu.VMEM((half_rows, d_model), BF16),
        pltpu.VMEM((half_rows, d_model), F32),
    ]

    return pl.pallas_call(
        kern,
        grid=(n_blocks + 1,),
        in_specs=[
            pl.BlockSpec((step_rows, d_model), lambda t: (jnp.minimum(t, n_blocks - 1), 0)),
            pl.BlockSpec((step_rows, d_model), lambda t: (jnp.maximum(t - 1, 0), 0)),
            _resident(w_main.shape),
            _resident((d_model, LANES)),
            _resident((CONV_WIDTH, d_conv)),
            _resident((1, d_conv)),
            _resident((1, LANES)),
            _resident((1, LANES)),
            _resident((1, d_ssd)),
            _resident((1, d_ssd)),
            _resident(w_pool.shape),
            _resident((1, d_pool)),
            _resident((d_mix, d_model)),
            _resident((1, d_model)),
            _resident((1, d_model)),
            _resident((CHUNK, 2 * CHUNK)),
        ],
        out_specs=pl.BlockSpec((step_rows, d_model), lambda t: (jnp.maximum(t - 1, 0), 0)),
        out_shape=jax.ShapeDtypeStruct((n_tok, d_model), F32),
        scratch_shapes=scratch,
        compiler_params=pltpu.CompilerParams(
            dimension_semantics=("arbitrary",), vmem_limit_bytes=VMEM_LIMIT_BYTES),
        name="mixer",
    )(x2d, x2d, w_main, w_dt_pad, conv_w, row(conv_b), dtb_pad, alog_pad, dskip_row, row(ssd_norm_w),
      w_pool.astype(BF16), row(pool_scale), w_out.astype(BF16), row(ln_g), row(ln_b), tri)


def _ffn_layer(h, w1, w2, ln_g, ln_b, *, alpha, tm, ff_block):
    n_tok, d_model = h.shape
    d_ff = w1.shape[1]
    assert n_tok % tm == 0 and d_ff % ff_block == 0
    kern = functools.partial(_ffn_kernel, alpha=alpha, ff_block=ff_block)
    return pl.pallas_call(
        kern,
        grid=(n_tok // tm,),
        in_specs=[
            pl.BlockSpec((tm, d_model), lambda i: (i, 0)),
            _resident((d_model, d_ff)),
            _resident((d_ff, d_model)),
            _resident((1, d_model)),
            _resident((1, d_model)),
        ],
        out_specs=pl.BlockSpec((tm, d_model), lambda i: (i, 0)),
        out_shape=jax.ShapeDtypeStruct(h.shape, F32),
        compiler_params=pltpu.CompilerParams(
            dimension_semantics=("arbitrary",), vmem_limit_bytes=VMEM_LIMIT_BYTES),
        name="ffn",
    )(h, w1.astype(BF16), w2.astype(BF16), ln_g[None, :], ln_b[None, :])


def kernel(x, w_in, w_pool, pool_scale, conv_w, conv_b, dt_bias, a_log, d_skip, ssd_norm_w, w_out,
           ln1_g, ln1_b, w_ff1, w_ff2, ln2_g, ln2_b):
    depth = w_in.shape[0]
    alpha = (2.0 * depth) ** 0.25
    bsz, seq, d_model = x.shape
    h = x
    for l in range(depth):
        h = _mixer_layer(h.reshape(bsz, seq, d_model), w_in[l], w_pool[l], pool_scale[l], conv_w[l], conv_b[l],
                         dt_bias[l], a_log[l], d_skip[l], ssd_norm_w[l], w_out[l], ln1_g[l], ln1_b[l], alpha=alpha)
        h = _ffn_layer(h, w_ff1[l], w_ff2[l], ln2_g[l], ln2_b[l], alpha=alpha, tm=FFN_ROWS, ff_block=FFN_COLS)
    return h.reshape(bsz, seq, d_model)
```

```python
import functools
import math

import jax
import jax.numpy as jnp
from jax import lax
from jax.experimental import pallas as pl
from jax.experimental.pallas import tpu as pltpu

POOL_WINDOWS = (2, 4, 8, 16)
N_POOL_GROUPS = len(POOL_WINDOWS)
SSD_HEAD_DIM = 64
N_BC_GROUPS = 4
D_STATE = 128
CONV_WIDTH = 4
CHUNK = 128
LN_EPS = 1e-5
RMS_EPS = 1e-5

LANES = 128
MXU_COLS = 256
CONV_HEAD = 8
POOL_HEAD = 16
DT_DUP_LANE = 32
HEADS_PER_SLAB = LANES // SSD_HEAD_DIM
VMEM_LIMIT_BYTES = 56 * 1024 * 1024
CHUNKS_PER_HALF = 2
CHUNKS_PER_STEP = 2 * CHUNKS_PER_HALF
FFN_ROWS = 1024
FFN_COLS = 1024
LOG2E = math.log2(math.e)

BF16 = jnp.bfloat16
F32 = jnp.float32


def _silu(v):
    return v * (1.0 / (1.0 + jnp.exp2(v * (-LOG2E))))


def _softplus(v):
    return jnp.maximum(v, 0.0) + jnp.log1p(jnp.exp(-jnp.abs(v)))


def _layer_norm(v, g, b):
    mu = jnp.mean(v, axis=-1, keepdims=True)
    d = v - mu
    var = jnp.mean(d * d, axis=-1, keepdims=True)
    return d * lax.rsqrt(var + LN_EPS) * g + b


def _dot(a, b):
    return jnp.dot(a, b, preferred_element_type=F32)


class _Interleaver:
    def __init__(self, tasks, points):
        self._tasks = list(tasks)
        self._total = len(self._tasks)
        self._points = points
        self._seen = 0
        self._done = 0

    def step(self):
        left = self._points - self._seen
        self._seen += 1
        target = self._done + -(-(self._total - self._done) // max(left, 1))
        while self._done < min(target, self._total):
            self._tasks[self._done]()
            self._done += 1

    def drain(self):
        while self._done < self._total:
            self._tasks[self._done]()
            self._done += 1


def _mixer_kernel(x_ref, xprev_ref, wmain_ref, wdt_ref, convw_ref, convb_ref, dtb_ref, alog_ref, dskip_ref,
                  normw_ref, wpool_ref, pscale_ref, wout_ref, g_ref, b_ref, tri_ref,
                  h_ref,
                  *scratch,
                  chunks_per_seq, d_pool, d_ssd, n_heads, alpha):
    t = pl.program_id(0)
    heads_per_group = n_heads // N_BC_GROUPS
    slabs_per_group = heads_per_group // HEADS_PER_SLAB
    group_width = d_ssd // N_BC_GROUPS
    x_slabs = d_ssd // LANES
    half_rows = CHUNKS_PER_HALF * CHUNK
    sets = [scratch[4 * i:4 * i + 4] for i in range(CHUNKS_PER_STEP)]
    mixed_refs = scratch[4 * CHUNKS_PER_STEP:4 * CHUNKS_PER_STEP + 2]
    state, xb_ref, hpre_ref = scratch[4 * CHUNKS_PER_STEP + 2:]

    @pl.when(t == 0)
    def _():
        for ref in scratch:
            ref[...] = jnp.zeros(ref.shape, ref.dtype)

    lane = lax.broadcasted_iota(jnp.int32, (1, LANES), 1)
    row_i = lax.broadcasted_iota(jnp.int32, (CHUNK, LANES), 0)
    col_i = lax.broadcasted_iota(jnp.int32, (CHUNK, LANES), 1)
    causal = row_i >= col_i
    lo_half = lane < SSD_HEAD_DIM
    a_neg2 = jnp.where(lane < n_heads, -LOG2E * jnp.exp(alog_ref[...]), 0.0)

    def project_tasks(r0, dsts):
        slabs_per_dot = MXU_COLS // LANES

        def cast():
            xb_ref[...] = x_ref[r0:r0 + half_rows, :].astype(BF16)

        def slab_task(which, col0, j0, head):
            def run():
                res = _dot(xb_ref[...], wmain_ref[:, col0 + j0 * LANES:col0 + (j0 + slabs_per_dot) * LANES])
                for ci, dst in enumerate(dsts):
                    for dj in range(slabs_per_dot):
                        dst[which][j0 + dj, head:head + CHUNK, :] = res[ci * CHUNK:(ci + 1) * CHUNK, dj * LANES:(dj + 1) * LANES]
            return run

        def gate_task(c0):
            def run():
                res = _dot(xb_ref[...], wmain_ref[:, d_pool + c0:d_pool + c0 + MXU_COLS])
                for ci, dst in enumerate(dsts):
                    dst[1][:, c0:c0 + MXU_COLS] = res[ci * CHUNK:(ci + 1) * CHUNK]
            return run

        def dt_task():
            res = _softplus(_dot(xb_ref[...], wdt_ref[...]) + dtb_ref[...])
            for ci, dst in enumerate(dsts):
                dst[3][...] = res[ci * CHUNK:(ci + 1) * CHUNK]

        n_conv_slabs, n_pool_slabs = dsts[0][2].shape[0], dsts[0][0].shape[0]
        tasks = [cast, dt_task]
        tasks += [slab_task(2, d_pool + d_ssd, j0, CONV_HEAD) for j0 in range(0, n_conv_slabs, slabs_per_dot)]
        tasks += [slab_task(0, 0, j0, POOL_HEAD) for j0 in range(0, n_pool_slabs, slabs_per_dot)]
        tasks += [gate_task(c0) for c0 in range(0, d_ssd, MXU_COLS)]
        return tasks

    def out_project_tasks(r0, mixed):
        d_model = wout_ref.shape[1]

        def block_task(c0):
            def run():
                hpre_ref[:, c0:c0 + MXU_COLS] = _dot(mixed[...], wout_ref[:, c0:c0 + MXU_COLS])
            return run

        def norm_task(ci):
            def run():
                rows = slice(ci * CHUNK, (ci + 1) * CHUNK)
                pre = alpha * xprev_ref[r0 + ci * CHUNK:r0 + (ci + 1) * CHUNK, :] + hpre_ref[rows, :]
                h_ref[r0 + ci * CHUNK:r0 + (ci + 1) * CHUNK, :] = _layer_norm(pre, g_ref[...], b_ref[...])
            return run

        return [block_task(c0) for c0 in range(0, d_model, MXU_COLS)] + [norm_task(ci) for ci in range(CHUNKS_PER_HALF)]

    def mix_chunk(chunk, cur, prev, mixed, m0, fill):
        ubuf, zbuf, xbuf, dtbuf = cur
        mrows = slice(m0, m0 + CHUNK)
        seq_chunk = lax.rem(chunk + chunks_per_seq, chunks_per_seq)
        is_start = seq_chunk == 0
        ubuf[:, 0:POOL_HEAD, :] = jnp.where(is_start, 0.0, prev[0][:, CHUNK:CHUNK + POOL_HEAD, :])
        xbuf[:, 0:CONV_HEAD, :] = jnp.where(is_start, 0.0, prev[2][:, CHUNK:CHUNK + CONV_HEAD, :])

        def conv_silu(j):
            cols = slice(j * LANES, (j + 1) * LANES)
            acc = convb_ref[:, cols] + xbuf[j, CONV_HEAD:CONV_HEAD + CHUNK, :] * convw_ref[CONV_WIDTH - 1:CONV_WIDTH, cols]
            for k in range(CONV_WIDTH - 1):
                off = CONV_HEAD - (CONV_WIDTH - 1) + k
                acc = acc + xbuf[j, off:off + CHUNK, :] * convw_ref[k:k + 1, cols]
            return _silu(acc)

        pos = seq_chunk * CHUNK + row_i + 1
        for gi, w in enumerate(POOL_WINDOWS):
            cur_u = ubuf[gi, POOL_HEAD:POOL_HEAD + CHUNK, :]
            win_sum = cur_u
            for j in range(1, w):
                win_sum = win_sum + ubuf[gi, POOL_HEAD - j:POOL_HEAD - j + CHUNK, :]
            cnt = jnp.minimum(pos, w).astype(F32)
            diff = win_sum / cnt - cur_u
            yp = _dot(diff.astype(BF16), wpool_ref[gi]) * pscale_ref[:, gi * LANES:(gi + 1) * LANES]
            mixed[mrows, gi * LANES:(gi + 1) * LANES] = yp.astype(BF16)

        dt_all = dtbuf[...]
        a = dt_all * a_neg2
        a_hi = a.astype(BF16)
        a_lo = (a - a_hi.astype(F32)).astype(BF16)
        acs = _dot(tri_ref[...], jnp.concatenate([a_hi, a_lo], axis=0))
        packed = acs + jnp.where(lane >= DT_DUP_LANE, dt_all, 0.0)
        packed_t = packed.T
        acs_t = packed_t[0:DT_DUP_LANE]
        dt_t = packed_t[DT_DUP_LANE:2 * DT_DUP_LANE]
        w_t = jnp.exp2(acs_t[:, CHUNK - 1:CHUNK] - acs_t) * dt_t
        src_t = acs_t - jnp.log2(dt_t)

        for g in range(N_BC_GROUPS):
            b_g = conv_silu(x_slabs + g)
            c_g = conv_silu(x_slabs + N_BC_GROUPS + g)
            c_bf = c_g.astype(BF16)
            b_t = b_g.T
            cb = _dot(c_bf, b_t.astype(BF16))
            gcol = g * group_width
            st_g = jnp.where(is_start, 0.0, state[:, gcol:gcol + group_width])
            y_off = _dot(c_bf, st_g.astype(BF16))
            fill.step()
            v_slabs = []
            ssq = jnp.zeros((CHUNK, 1), F32)
            for q in range(slabs_per_group):
                col0 = gcol + q * LANES
                h0 = g * heads_per_group + q * HEADS_PER_SLAB
                xq = conv_silu(g * slabs_per_group + q)
                xq_bf = xq.astype(BF16)
                zero = jnp.zeros_like(xq_bf)
                rhs = jnp.concatenate([jnp.where(lo_half, xq_bf, zero), jnp.where(lo_half, zero, xq_bf)], axis=0)
                m_parts, bw_parts, e_parts = [], [], []
                for h in (h0, h0 + 1):
                    bc = jnp.broadcast_to(acs[:, h:h + 1], (CHUNK, LANES))
                    dec_dt = jnp.exp2(jnp.where(causal, bc - src_t[h:h + 1, :], -jnp.inf))
                    m_parts.append((cb * dec_dt).astype(BF16))
                    bw_parts.append((b_t * w_t[h:h + 1, :]).astype(BF16))
                    e_parts.append(jnp.exp2(bc))
                lhs = jnp.concatenate([jnp.concatenate(m_parts, axis=1), jnp.concatenate(bw_parts, axis=1)], axis=0)
                res = _dot(lhs, rhs)
                e_pair = jnp.where(lo_half, e_parts[0], e_parts[1])
                y = res[0:CHUNK] + y_off[:, q * LANES:(q + 1) * LANES] * e_pair + dskip_ref[:, col0:col0 + LANES] * xq
                state[:, col0:col0 + LANES] = (st_g[:, q * LANES:(q + 1) * LANES] * e_pair[CHUNK - 1:CHUNK, :]
                                               + res[CHUNK:2 * CHUNK])
                v = y * _silu(zbuf[:, col0:col0 + LANES])
                ssq = ssq + jnp.sum(v * v, axis=-1, keepdims=True)
                v_slabs.append(v)
                fill.step()
            scale = lax.rsqrt(ssq * (1.0 / group_width) + RMS_EPS)
            for q in range(slabs_per_group):
                col0 = gcol + q * LANES
                vn = v_slabs[q] * scale * normw_ref[:, col0:col0 + LANES]
                mixed[mrows, d_pool + col0:d_pool + col0 + LANES] = vn.astype(BF16)

    points_per_chunk = N_BC_GROUPS * (1 + slabs_per_group)
    first = CHUNKS_PER_STEP * t
    for half in range(2):
        r0 = half * half_rows
        new_sets = [sets[half * CHUNKS_PER_HALF + i] for i in range(CHUNKS_PER_HALF)]
        tasks = out_project_tasks(r0, mixed_refs[half]) + project_tasks(r0, new_sets)
        fill = _Interleaver(tasks, CHUNKS_PER_HALF * points_per_chunk)
        for i in range(CHUNKS_PER_HALF):
            local = (half * CHUNKS_PER_HALF - CHUNKS_PER_HALF + i) % CHUNKS_PER_STEP
            chunk = first + half * CHUNKS_PER_HALF - CHUNKS_PER_HALF + i
            mix_chunk(chunk, sets[local], sets[(local - 1) % CHUNKS_PER_STEP],
                      mixed_refs[local // CHUNKS_PER_HALF], (local % CHUNKS_PER_HALF) * CHUNK, fill)
        fill.drain()


def _ffn_kernel(h_ref, w1_ref, w2_ref, g_ref, b_ref, o_ref, *, alpha, ff_block):
    h = h_ref[...]
    hb = h.astype(BF16)
    d_ff = w1_ref.shape[1]
    acc = alpha * h
    for j in range(d_ff // ff_block):
        hid = _dot(hb, w1_ref[:, j * ff_block:(j + 1) * ff_block])
        act = jnp.square(jnp.maximum(hid, 0.0)).astype(BF16)
        acc = acc + _dot(act, w2_ref[j * ff_block:(j + 1) * ff_block, :])
    o_ref[...] = _layer_norm(acc, g_ref[...], b_ref[...])


def _resident(shape):
    nd = len(shape)
    return pl.BlockSpec(shape, lambda *_: (0,) * nd, pipeline_mode=pl.Buffered(1))


def _mixer_layer(x, w_in, w_pool, pool_scale, conv_w, conv_b, dt_bias, a_log, d_skip, ssd_norm_w,
                 w_out, ln_g, ln_b, *, alpha):
    bsz, seq, d_model = x.shape
    n_tok = bsz * seq
    step_rows = CHUNKS_PER_STEP * CHUNK
    half_rows = CHUNKS_PER_HALF * CHUNK
    n_blocks = n_tok // step_rows
    d_pool = w_pool.shape[0] * w_pool.shape[1]
    n_heads = dt_bias.shape[0]
    d_ssd = n_heads * SSD_HEAD_DIM
    d_conv = conv_w.shape[1]
    d_main = d_pool + d_ssd + d_conv
    d_mix = d_pool + d_ssd
    assert w_in.shape[1] == d_main + n_heads and d_conv == d_ssd + 2 * N_BC_GROUPS * D_STATE
    assert n_heads <= DT_DUP_LANE and DT_DUP_LANE + n_heads <= LANES
    assert n_tok % step_rows == 0 and seq % CHUNK == 0 and w_pool.shape[1] == LANES

    x2d = x.reshape(n_tok, d_model)
    w_main = w_in.astype(BF16)
    w_dt = w_in[:, d_main:]
    w_dt_pad = jnp.zeros((d_model, LANES), F32)
    w_dt_pad = w_dt_pad.at[:, :n_heads].set(w_dt).at[:, DT_DUP_LANE:DT_DUP_LANE + n_heads].set(w_dt).astype(BF16)
    dtb_pad = jnp.zeros((1, LANES), F32).at[0, :n_heads].set(dt_bias).at[0, DT_DUP_LANE:DT_DUP_LANE + n_heads].set(dt_bias)
    alog_pad = jnp.zeros((1, LANES), F32).at[0, :n_heads].set(a_log)
    dskip_row = jnp.repeat(d_skip, SSD_HEAD_DIM)[None, :]
    tri = (jnp.arange(CHUNK)[:, None] >= (jnp.arange(2 * CHUNK)[None, :] % CHUNK)).astype(BF16)

    kern = functools.partial(_mixer_kernel, chunks_per_seq=seq // CHUNK, d_pool=d_pool, d_ssd=d_ssd,
                             n_heads=n_heads, alpha=alpha)
    row = lambda v: v[None, :]

    def chunk_set():
        return [
            pltpu.VMEM((d_pool // LANES, POOL_HEAD + CHUNK, LANES), F32),
            pltpu.VMEM((CHUNK, d_ssd), F32),
            pltpu.VMEM((d_conv // LANES, CONV_HEAD + CHUNK, LANES), F32),
            pltpu.VMEM((CHUNK, LANES), F32),
        ]

    scratch = []
    for _ in range(CHUNKS_PER_STEP):
        scratch += chunk_set()
    scratch += [
        pltpu.VMEM((half_rows, d_mix), BF16),
        pltpu.VMEM((half_rows, d_mix), BF16),
        pltpu.VMEM((D_STATE, d_ssd), F32),
        pltpu.VMEM((half_rows, d_model), BF16),
        pltpu.VMEM((half_rows, d_model), F32),
    ]

    return pl.pallas_call(
        kern,
        grid=(n_blocks + 1,),
        in_specs=[
            pl.BlockSpec((step_rows, d_model), lambda t: (jnp.minimum(t, n_blocks - 1), 0)),
            pl.BlockSpec((step_rows, d_model), lambda t: (jnp.maximum(t - 1, 0), 0)),
            _resident(w_main.shape),
            _resident((d_model, LANES)),
            _resident((CONV_WIDTH, d_conv)),
            _resident((1, d_conv)),
            _resident((1, LANES)),
            _resident((1, LANES)),
            _resident((1, d_ssd)),
            _resident((1, d_ssd)),
            _resident(w_pool.shape),
            _resident((1, d_pool)),
            _resident((d_mix, d_model)),
            _resident((1, d_model)),
            _resident((1, d_model)),
            _resident((CHUNK, 2 * CHUNK)),
        ],
        out_specs=pl.BlockSpec((step_rows, d_model), lambda t: (jnp.maximum(t - 1, 0), 0)),
        out_shape=jax.ShapeDtypeStruct((n_tok, d_model), F32),
        scratch_shapes=scratch,
        compiler_params=pltpu.CompilerParams(
            dimension_semantics=("arbitrary",), vmem_limit_bytes=VMEM_LIMIT_BYTES),
        name="mixer",
    )(x2d, x2d, w_main, w_dt_pad, conv_w, row(conv_b), dtb_pad, alog_pad, dskip_row, row(ssd_norm_w),
      w_pool.astype(BF16), row(pool_scale), w_out.astype(BF16), row(ln_g), row(ln_b), tri)


def _ffn_layer(h, w1, w2, ln_g, ln_b, *, alpha, tm, ff_block):
    n_tok, d_model = h.shape
    d_ff = w1.shape[1]
    assert n_tok % tm == 0 and d_ff % ff_block == 0
    kern = functools.partial(_ffn_kernel, alpha=alpha, ff_block=ff_block)
    return pl.pallas_call(
        kern,
        grid=(n_tok // tm,),
        in_specs=[
            pl.BlockSpec((tm, d_model), lambda i: (i, 0)),
            _resident((d_model, d_ff)),
            _resident((d_ff, d_model)),
            _resident((1, d_model)),
            _resident((1, d_model)),
        ],
        out_specs=pl.BlockSpec((tm, d_model), lambda i: (i, 0)),
        out_shape=jax.ShapeDtypeStruct(h.shape, F32),
        compiler_params=pltpu.CompilerParams(
            dimension_semantics=("arbitrary",), vmem_limit_bytes=VMEM_LIMIT_BYTES),
        name="ffn",
    )(h, w1.astype(BF16), w2.astype(BF16), ln_g[None, :], ln_b[None, :])


def kernel(x, w_in, w_pool, pool_scale, conv_w, conv_b, dt_bias, a_log, d_skip, ssd_norm_w, w_out,
           ln1_g, ln1_b, w_ff1, w_ff2, ln2_g, ln2_b):
    depth = w_in.shape[0]
    alpha = (2.0 * depth) ** 0.25
    bsz, seq, d_model = x.shape
    h = x
    for l in range(depth):
        h = _mixer_layer(h.reshape(bsz, seq, d_model), w_in[l], w_pool[l], pool_scale[l], conv_w[l], conv_b[l],
                         dt_bias[l], a_log[l], d_skip[l], ssd_norm_w[l], w_out[l], ln1_g[l], ln1_b[l], alpha=alpha)
        h = _ffn_layer(h, w_ff1[l], w_ff2[l], ln2_g[l], ln2_b[l], alpha=alpha, tm=FFN_ROWS, ff_block=FFN_COLS)
    return h.reshape(bsz, seq, d_model)
```

```python
import functools
import math

import jax
import jax.numpy as jnp
from jax import lax
from jax.experimental import pallas as pl
from jax.experimental.pallas import tpu as pltpu

POOL_WINDOWS = (2, 4, 8, 16)
N_POOL_GROUPS = len(POOL_WINDOWS)
SSD_HEAD_DIM = 64
N_BC_GROUPS = 4
D_STATE = 128
CONV_WIDTH = 4
CHUNK = 128
LN_EPS = 1e-5
RMS_EPS = 1e-5

LANES = 128
MXU_COLS = 256
CONV_HEAD = 8
POOL_HEAD = 16
DT_DUP_LANE = 32
HEADS_PER_SLAB = LANES // SSD_HEAD_DIM
VMEM_LIMIT_BYTES = 56 * 1024 * 1024
CHUNKS_PER_HALF = 2
CHUNKS_PER_STEP = 2 * CHUNKS_PER_HALF
FFN_ROWS = 1024
FFN_COLS = 1024
LOG2E = math.log2(math.e)

BF16 = jnp.bfloat16
F32 = jnp.float32


def _silu(v):
    return v * (1.0 / (1.0 + jnp.exp2(v * (-LOG2E))))


def _softplus(v):
    return jnp.maximum(v, 0.0) + jnp.log1p(jnp.exp(-jnp.abs(v)))


def _layer_norm(v, g, b):
    mu = jnp.mean(v, axis=-1, keepdims=True)
    d = v - mu
    var = jnp.mean(d * d, axis=-1, keepdims=True)
    return d * lax.rsqrt(var + LN_EPS) * g + b


def _dot(a, b):
    return jnp.dot(a, b, preferred_element_type=F32)


class _Interleaver:
    def __init__(self, tasks, points):
        self._tasks = list(tasks)
        self._total = len(self._tasks)
        self._points = points
        self._seen = 0
        self._done = 0

    def step(self):
        self._seen += 1
        target = min(self._total, -(-self._seen * self._total // self._points))
        while self._done < target:
            self._tasks[self._done]()
            self._done += 1

    def drain(self):
        while self._done < self._total:
            self._tasks[self._done]()
            self._done += 1


def _mixer_kernel(x_ref, xprev_ref, wmain_ref, wdt_ref, convw_ref, convb_ref, dtb_ref, alog_ref, dskip_ref,
                  normw_ref, wpool_ref, pscale_ref, wout_ref, g_ref, b_ref, tri_ref,
                  h_ref,
                  *scratch,
                  chunks_per_seq, d_pool, d_ssd, n_heads, alpha):
    t = pl.program_id(0)
    heads_per_group = n_heads // N_BC_GROUPS
    slabs_per_group = heads_per_group // HEADS_PER_SLAB
    group_width = d_ssd // N_BC_GROUPS
    x_slabs = d_ssd // LANES
    half_rows = CHUNKS_PER_HALF * CHUNK
    sets = [scratch[4 * i:4 * i + 4] for i in range(CHUNKS_PER_STEP)]
    mixed_refs = scratch[4 * CHUNKS_PER_STEP:4 * CHUNKS_PER_STEP + 2]
    state, xb_ref, hpre_ref = scratch[4 * CHUNKS_PER_STEP + 2:]

    @pl.when(t == 0)
    def _():
        for ref in scratch:
            ref[...] = jnp.zeros(ref.shape, ref.dtype)

    lane = lax.broadcasted_iota(jnp.int32, (1, LANES), 1)
    row_i = lax.broadcasted_iota(jnp.int32, (CHUNK, LANES), 0)
    col_i = lax.broadcasted_iota(jnp.int32, (CHUNK, LANES), 1)
    causal = row_i >= col_i
    lo_half = lane < SSD_HEAD_DIM
    a_neg2 = jnp.where(lane < n_heads, -LOG2E * jnp.exp(alog_ref[...]), 0.0)

    def project_tasks(r0, dsts):
        slabs_per_dot = MXU_COLS // LANES

        def cast():
            xb_ref[...] = x_ref[r0:r0 + half_rows, :].astype(BF16)

        def slab_task(which, col0, j0, head):
            def run():
                res = _dot(xb_ref[...], wmain_ref[:, col0 + j0 * LANES:col0 + (j0 + slabs_per_dot) * LANES])
                for ci, dst in enumerate(dsts):
                    for dj in range(slabs_per_dot):
                        dst[which][j0 + dj, head:head + CHUNK, :] = res[ci * CHUNK:(ci + 1) * CHUNK, dj * LANES:(dj + 1) * LANES]
            return run

        def gate_task(c0):
            def run():
                res = _dot(xb_ref[...], wmain_ref[:, d_pool + c0:d_pool + c0 + MXU_COLS])
                for ci, dst in enumerate(dsts):
                    dst[1][:, c0:c0 + MXU_COLS] = res[ci * CHUNK:(ci + 1) * CHUNK]
            return run

        def dt_task():
            res = _softplus(_dot(xb_ref[...], wdt_ref[...]) + dtb_ref[...])
            for ci, dst in enumerate(dsts):
                dst[3][...] = res[ci * CHUNK:(ci + 1) * CHUNK]

        n_conv_slabs, n_pool_slabs = dsts[0][2].shape[0], dsts[0][0].shape[0]
        tasks = [cast, dt_task]
        tasks += [slab_task(2, d_pool + d_ssd, j0, CONV_HEAD) for j0 in range(0, n_conv_slabs, slabs_per_dot)]
        tasks += [slab_task(0, 0, j0, POOL_HEAD) for j0 in range(0, n_pool_slabs, slabs_per_dot)]
        tasks += [gate_task(c0) for c0 in range(0, d_ssd, MXU_COLS)]
        return tasks

    def out_project_tasks(r0, mixed):
        d_model = wout_ref.shape[1]

        def block_task(c0):
            def run():
                hpre_ref[:, c0:c0 + MXU_COLS] = _dot(mixed[...], wout_ref[:, c0:c0 + MXU_COLS])
            return run

        def norm_task(ci):
            def run():
                rows = slice(ci * CHUNK, (ci + 1) * CHUNK)
                pre = alpha * xprev_ref[r0 + ci * CHUNK:r0 + (ci + 1) * CHUNK, :] + hpre_ref[rows, :]
                h_ref[r0 + ci * CHUNK:r0 + (ci + 1) * CHUNK, :] = _layer_norm(pre, g_ref[...], b_ref[...])
            return run

        return [block_task(c0) for c0 in range(0, d_model, MXU_COLS)] + [norm_task(ci) for ci in range(CHUNKS_PER_HALF)]

    def mix_chunk(chunk, cur, prev, mixed, m0, fill):
        ubuf, zbuf, xbuf, dtbuf = cur
        mrows = slice(m0, m0 + CHUNK)
        seq_chunk = lax.rem(chunk + chunks_per_seq, chunks_per_seq)
        is_start = seq_chunk == 0
        ubuf[:, 0:POOL_HEAD, :] = jnp.where(is_start, 0.0, prev[0][:, CHUNK:CHUNK + POOL_HEAD, :])
        xbuf[:, 0:CONV_HEAD, :] = jnp.where(is_start, 0.0, prev[2][:, CHUNK:CHUNK + CONV_HEAD, :])

        def conv_silu(j):
            cols = slice(j * LANES, (j + 1) * LANES)
            acc = convb_ref[:, cols] + xbuf[j, CONV_HEAD:CONV_HEAD + CHUNK, :] * convw_ref[CONV_WIDTH - 1:CONV_WIDTH, cols]
            for k in range(CONV_WIDTH - 1):
                off = CONV_HEAD - (CONV_WIDTH - 1) + k
                acc = acc + xbuf[j, off:off + CHUNK, :] * convw_ref[k:k + 1, cols]
            return _silu(acc)

        pos = seq_chunk * CHUNK + row_i + 1
        for gi, w in enumerate(POOL_WINDOWS):
            cur_u = ubuf[gi, POOL_HEAD:POOL_HEAD + CHUNK, :]
            win_sum = cur_u
            for j in range(1, w):
                win_sum = win_sum + ubuf[gi, POOL_HEAD - j:POOL_HEAD - j + CHUNK, :]
            cnt = jnp.minimum(pos, w).astype(F32)
            diff = win_sum / cnt - cur_u
            yp = _dot(diff.astype(BF16), wpool_ref[gi]) * pscale_ref[:, gi * LANES:(gi + 1) * LANES]
            mixed[mrows, gi * LANES:(gi + 1) * LANES] = yp.astype(BF16)

        dt_all = dtbuf[...]
        a = dt_all * a_neg2
        a_hi = a.astype(BF16)
        a_lo = (a - a_hi.astype(F32)).astype(BF16)
        acs = _dot(tri_ref[...], jnp.concatenate([a_hi, a_lo], axis=0))
        packed = acs + jnp.where(lane >= DT_DUP_LANE, dt_all, 0.0)
        packed_t = packed.T
        acs_t = packed_t[0:DT_DUP_LANE]
        dt_t = packed_t[DT_DUP_LANE:2 * DT_DUP_LANE]
        w_t = jnp.exp2(acs_t[:, CHUNK - 1:CHUNK] - acs_t) * dt_t
        src_t = acs_t - jnp.log2(dt_t)

        for g in range(N_BC_GROUPS):
            b_g = conv_silu(x_slabs + g)
            c_g = conv_silu(x_slabs + N_BC_GROUPS + g)
            b_bf = b_g.astype(BF16)
            c_bf = c_g.astype(BF16)
            b_t = b_g.T
            cb = lax.dot_general(c_bf, b_bf, (((1,), (1,)), ((), ())), preferred_element_type=F32)
            gcol = g * group_width
            st_g = jnp.where(is_start, 0.0, state[:, gcol:gcol + group_width])
            y_off = _dot(c_bf, st_g.astype(BF16))
            fill.step()
            v_slabs = []
            ssq = jnp.zeros((CHUNK, 1), F32)
            for q in range(slabs_per_group):
                col0 = gcol + q * LANES
                h0 = g * heads_per_group + q * HEADS_PER_SLAB
                xq = conv_silu(g * slabs_per_group + q)
                xq_bf = xq.astype(BF16)
                zero = jnp.zeros_like(xq_bf)
                rhs = jnp.concatenate([jnp.where(lo_half, xq_bf, zero), jnp.where(lo_half, zero, xq_bf)], axis=0)
                m_parts, bw_parts, e_parts = [], [], []
                for h in (h0, h0 + 1):
                    bc = jnp.broadcast_to(acs[:, h:h + 1], (CHUNK, LANES))
                    dec_dt = jnp.exp2(jnp.where(causal, bc - src_t[h:h + 1, :], -jnp.inf))
                    m_parts.append((cb * dec_dt).astype(BF16))
                    bw_parts.append((b_t * w_t[h:h + 1, :]).astype(BF16))
                    e_parts.append(jnp.exp2(bc))
                lhs = jnp.concatenate([jnp.concatenate(m_parts, axis=1), jnp.concatenate(bw_parts, axis=1)], axis=0)
                res = _dot(lhs, rhs)
                e_pair = jnp.where(lo_half, e_parts[0], e_parts[1])
                y = res[0:CHUNK] + y_off[:, q * LANES:(q + 1) * LANES] * e_pair + dskip_ref[:, col0:col0 + LANES] * xq
                state[:, col0:col0 + LANES] = (st_g[:, q * LANES:(q + 1) * LANES] * e_pair[CHUNK - 1:CHUNK, :]
                                               + res[CHUNK:2 * CHUNK])
                v = y * _silu(zbuf[:, col0:col0 + LANES])
                ssq = ssq + jnp.sum(v * v, axis=-1, keepdims=True)
                v_slabs.append(v)
                fill.step()
            scale = lax.rsqrt(ssq * (1.0 / group_width) + RMS_EPS)
            for q in range(slabs_per_group):
                col0 = gcol + q * LANES
                vn = v_slabs[q] * scale * normw_ref[:, col0:col0 + LANES]
                mixed[mrows, d_pool + col0:d_pool + col0 + LANES] = vn.astype(BF16)

    points_per_chunk = N_BC_GROUPS * (1 + slabs_per_group)
    first = CHUNKS_PER_STEP * t
    for half in range(2):
        r0 = half * half_rows
        new_sets = [sets[half * CHUNKS_PER_HALF + i] for i in range(CHUNKS_PER_HALF)]
        out_mixed = mixed_refs[half]
        fill = _Interleaver(out_project_tasks(r0, out_mixed) + project_tasks(r0, new_sets),
                            CHUNKS_PER_HALF * points_per_chunk)
        for i in range(CHUNKS_PER_HALF):
            local = (half * CHUNKS_PER_HALF - CHUNKS_PER_HALF + i) % CHUNKS_PER_STEP
            chunk = first + half * CHUNKS_PER_HALF - CHUNKS_PER_HALF + i
            mix_chunk(chunk, sets[local], sets[(local - 1) % CHUNKS_PER_STEP],
                      mixed_refs[local // CHUNKS_PER_HALF], (local % CHUNKS_PER_HALF) * CHUNK, fill)
        fill.drain()


def _ffn_kernel(h_ref, w1_ref, w2_ref, g_ref, b_ref, o_ref, *, alpha, ff_block):
    h = h_ref[...]
    hb = h.astype(BF16)
    d_ff = w1_ref.shape[1]
    acc = alpha * h
    for j in range(d_ff // ff_block):
        hid = _dot(hb, w1_ref[:, j * ff_block:(j + 1) * ff_block])
        act = jnp.square(jnp.maximum(hid, 0.0)).astype(BF16)
        acc = acc + _dot(act, w2_ref[j * ff_block:(j + 1) * ff_block, :])
    o_ref[...] = _layer_norm(acc, g_ref[...], b_ref[...])


def _resident(shape):
    nd = len(shape)
    return pl.BlockSpec(shape, lambda *_: (0,) * nd, pipeline_mode=pl.Buffered(1))


def _mixer_layer(x, w_in, w_pool, pool_scale, conv_w, conv_b, dt_bias, a_log, d_skip, ssd_norm_w,
                 w_out, ln_g, ln_b, *, alpha):
    bsz, seq, d_model = x.shape
    n_tok = bsz * seq
    step_rows = CHUNKS_PER_STEP * CHUNK
    half_rows = CHUNKS_PER_HALF * CHUNK
    n_blocks = n_tok // step_rows
    d_pool = w_pool.shape[0] * w_pool.shape[1]
    n_heads = dt_bias.shape[0]
    d_ssd = n_heads * SSD_HEAD_DIM
    d_conv = conv_w.shape[1]
    d_main = d_pool + d_ssd + d_conv
    d_mix = d_pool + d_ssd
    assert w_in.shape[1] == d_main + n_heads and d_conv == d_ssd + 2 * N_BC_GROUPS * D_STATE
    assert n_heads <= DT_DUP_LANE and DT_DUP_LANE + n_heads <= LANES
    assert n_tok % step_rows == 0 and seq % CHUNK == 0 and w_pool.shape[1] == LANES

    x2d = x.reshape(n_tok, d_model)
    w_main = w_in.astype(BF16)
    w_dt = w_in[:, d_main:]
    w_dt_pad = jnp.zeros((d_model, LANES), F32)
    w_dt_pad = w_dt_pad.at[:, :n_heads].set(w_dt).at[:, DT_DUP_LANE:DT_DUP_LANE + n_heads].set(w_dt).astype(BF16)
    dtb_pad = jnp.zeros((1, LANES), F32).at[0, :n_heads].set(dt_bias).at[0, DT_DUP_LANE:DT_DUP_LANE + n_heads].set(dt_bias)
    alog_pad = jnp.zeros((1, LANES), F32).at[0, :n_heads].set(a_log)
    dskip_row = jnp.repeat(d_skip, SSD_HEAD_DIM)[None, :]
    tri = (jnp.arange(CHUNK)[:, None] >= (jnp.arange(2 * CHUNK)[None, :] % CHUNK)).astype(BF16)

    kern = functools.partial(_mixer_kernel, chunks_per_seq=seq // CHUNK, d_pool=d_pool, d_ssd=d_ssd,
                             n_heads=n_heads, alpha=alpha)
    row = lambda v: v[None, :]

    def chunk_set():
        return [
            pltpu.VMEM((d_pool // LANES, POOL_HEAD + CHUNK, LANES), F32),
            pltpu.VMEM((CHUNK, d_ssd), F32),
            pltpu.VMEM((d_conv // LANES, CONV_HEAD + CHUNK, LANES), F32),
            pltpu.VMEM((CHUNK, LANES), F32),
        ]

    scratch = []
    for _ in range(CHUNKS_PER_STEP):
        scratch += chunk_set()
    scratch += [
        pltpu.VMEM((half_rows, d_mix), BF16),
        pltpu.VMEM((half_rows, d_mix), BF16),
        pltpu.VMEM((D_STATE, d_ssd), F32),
        pltpu.VMEM((half_rows, d_model), BF16),
        pltpu.VMEM((half_rows, d_model), F32),
    ]

    return pl.pallas_call(
        kern,
        grid=(n_blocks + 1,),
        in_specs=[
            pl.BlockSpec((step_rows, d_model), lambda t: (jnp.minimum(t, n_blocks - 1), 0)),
            pl.BlockSpec((step_rows, d_model), lambda t: (jnp.maximum(t - 1, 0), 0)),
            _resident(w_main.shape),
            _resident((d_model, LANES)),
            _resident((CONV_WIDTH, d_conv)),
            _resident((1, d_conv)),
            _resident((1, LANES)),
            _resident((1, LANES)),
            _resident((1, d_ssd)),
            _resident((1, d_ssd)),
            _resident(w_pool.shape),
            _resident((1, d_pool)),
            _resident((d_mix, d_model)),
            _resident((1, d_model)),
            _resident((1, d_model)),
            _resident((CHUNK, 2 * CHUNK)),
        ],
        out_specs=pl.BlockSpec((step_rows, d_model), lambda t: (jnp.maximum(t - 1, 0), 0)),
        out_shape=jax.ShapeDtypeStruct((n_tok, d_model), F32),
        scratch_shapes=scratch,
        compiler_params=pltpu.CompilerParams(
            dimension_semantics=("arbitrary",), vmem_limit_bytes=VMEM_LIMIT_BYTES),
        name="mixer",
    )(x2d, x2d, w_main, w_dt_pad, conv_w, row(conv_b), dtb_pad, alog_pad, dskip_row, row(ssd_norm_w),
      w_pool.astype(BF16), row(pool_scale), w_out.astype(BF16), row(ln_g), row(ln_b), tri)


def _ffn_layer(h, w1, w2, ln_g, ln_b, *, alpha, tm, ff_block):
    n_tok, d_model = h.shape
    d_ff = w1.shape[1]
    assert n_tok % tm == 0 and d_ff % ff_block == 0
    kern = functools.partial(_ffn_kernel, alpha=alpha, ff_block=ff_block)
    return pl.pallas_call(
        kern,
        grid=(n_tok // tm,),
        in_specs=[
            pl.BlockSpec((tm, d_model), lambda i: (i, 0)),
            _resident((d_model, d_ff)),
            _resident((d_ff, d_model)),
            _resident((1, d_model)),
            _resident((1, d_model)),
        ],
        out_specs=pl.BlockSpec((tm, d_model), lambda i: (i, 0)),
        out_shape=jax.ShapeDtypeStruct(h.shape, F32),
        compiler_params=pltpu.CompilerParams(
            dimension_semantics=("arbitrary",), vmem_limit_bytes=VMEM_LIMIT_BYTES),
        name="ffn",
    )(h, w1.astype(BF16), w2.astype(BF16), ln_g[None, :], ln_b[None, :])


def kernel(x, w_in, w_pool, pool_scale, conv_w, conv_b, dt_bias, a_log, d_skip, ssd_norm_w, w_out,
           ln1_g, ln1_b, w_ff1, w_ff2, ln2_g, ln2_b):
    depth = w_in.shape[0]
    alpha = (2.0 * depth) ** 0.25
    bsz, seq, d_model = x.shape
    h = x
    for l in range(depth):
        h = _mixer_layer(h.reshape(bsz, seq, d_model), w_in[l], w_pool[l], pool_scale[l], conv_w[l], conv_b[l],
                         dt_bias[l], a_log[l], d_skip[l], ssd_norm_w[l], w_out[l], ln1_g[l], ln1_b[l], alpha=alpha)
        h = _ffn_layer(h, w_ff1[l], w_ff2[l], ln2_g[l], ln2_b[l], alpha=alpha, tm=FFN_ROWS, ff_block=FFN_COLS)
    return h.reshape(bsz, seq, d_model)
```

```python
import functools
import math

import jax
import jax.numpy as jnp
from jax import lax
from jax.experimental import pallas as pl
from jax.experimental.pallas import tpu as pltpu

POOL_WINDOWS = (2, 4, 8, 16)
N_POOL_GROUPS = len(POOL_WINDOWS)
SSD_HEAD_DIM = 64
N_BC_GROUPS = 4
D_STATE = 128
CONV_WIDTH = 4
CHUNK = 128
LN_EPS = 1e-5
RMS_EPS = 1e-5

LANES = 128
MXU_COLS = 256
CONV_HEAD = 8
POOL_HEAD = 16
DT_DUP_LANE = 32
HEADS_PER_SLAB = LANES // SSD_HEAD_DIM
VMEM_LIMIT_BYTES = 56 * 1024 * 1024
CHUNKS_PER_HALF = 2
CHUNKS_PER_STEP = 2 * CHUNKS_PER_HALF
FFN_ROWS = 1024
FFN_COLS = 1024
LOG2E = math.log2(math.e)

BF16 = jnp.bfloat16
F32 = jnp.float32


def _silu(v):
    return v * (1.0 / (1.0 + jnp.exp2(v * (-LOG2E))))


def _softplus(v):
    return jnp.maximum(v, 0.0) + jnp.log1p(jnp.exp(-jnp.abs(v)))


def _layer_norm(v, g, b):
    mu = jnp.mean(v, axis=-1, keepdims=True)
    d = v - mu
    var = jnp.mean(d * d, axis=-1, keepdims=True)
    return d * lax.rsqrt(var + LN_EPS) * g + b


def _dot(a, b):
    return jnp.dot(a, b, preferred_element_type=F32)


class _Interleaver:
    def __init__(self, tasks, points):
        self._tasks = list(tasks)
        self._total = len(self._tasks)
        self._points = points
        self._seen = 0
        self._done = 0

    def step(self):
        left = self._points - self._seen
        self._seen += 1
        target = self._done + -(-(self._total - self._done) // max(left, 1))
        while self._done < min(target, self._total):
            self._tasks[self._done]()
            self._done += 1

    def drain(self):
        while self._done < self._total:
            self._tasks[self._done]()
            self._done += 1


def _mixer_kernel(x_ref, xprev_ref, wmain_ref, wdt_ref, convw_ref, convb_ref, dtb_ref, alog_ref, dskip_ref,
                  normw_ref, wpool_ref, pscale_ref, wout_ref, g_ref, b_ref, tri_ref,
                  h_ref,
                  *scratch,
                  chunks_per_seq, d_pool, d_ssd, n_heads, alpha):
    t = pl.program_id(0)
    heads_per_group = n_heads // N_BC_GROUPS
    slabs_per_group = heads_per_group // HEADS_PER_SLAB
    group_width = d_ssd // N_BC_GROUPS
    x_slabs = d_ssd // LANES
    half_rows = CHUNKS_PER_HALF * CHUNK
    sets = [scratch[4 * i:4 * i + 4] for i in range(CHUNKS_PER_STEP)]
    mixed_refs = scratch[4 * CHUNKS_PER_STEP:4 * CHUNKS_PER_STEP + 2]
    state, xb_ref, hpre_ref = scratch[4 * CHUNKS_PER_STEP + 2:]

    @pl.when(t == 0)
    def _():
        for ref in scratch:
            ref[...] = jnp.zeros(ref.shape, ref.dtype)

    lane = lax.broadcasted_iota(jnp.int32, (1, LANES), 1)
    row_i = lax.broadcasted_iota(jnp.int32, (CHUNK, LANES), 0)
    col_i = lax.broadcasted_iota(jnp.int32, (CHUNK, LANES), 1)
    causal = row_i >= col_i
    lo_half = lane < SSD_HEAD_DIM
    a_neg2 = jnp.where(lane < n_heads, -LOG2E * jnp.exp(alog_ref[...]), 0.0)

    def project_tasks(r0, dsts):
        slabs_per_dot = MXU_COLS // LANES

        def cast():
            xb_ref[...] = x_ref[r0:r0 + half_rows, :].astype(BF16)

        def slab_task(which, col0, j0, head):
            def run():
                res = _dot(xb_ref[...], wmain_ref[:, col0 + j0 * LANES:col0 + (j0 + slabs_per_dot) * LANES])
                for ci, dst in enumerate(dsts):
                    for dj in range(slabs_per_dot):
                        dst[which][j0 + dj, head:head + CHUNK, :] = res[ci * CHUNK:(ci + 1) * CHUNK, dj * LANES:(dj + 1) * LANES]
            return run

        def gate_task(c0):
            def run():
                res = _dot(xb_ref[...], wmain_ref[:, d_pool + c0:d_pool + c0 + MXU_COLS])
                for ci, dst in enumerate(dsts):
                    dst[1][:, c0:c0 + MXU_COLS] = res[ci * CHUNK:(ci + 1) * CHUNK]
            return run

        def dt_task():
            res = _softplus(_dot(xb_ref[...], wdt_ref[...]) + dtb_ref[...])
            for ci, dst in enumerate(dsts):
                dst[3][...] = res[ci * CHUNK:(ci + 1) * CHUNK]

        n_conv_slabs, n_pool_slabs = dsts[0][2].shape[0], dsts[0][0].shape[0]
        tasks = [cast, dt_task]
        tasks += [slab_task(2, d_pool + d_ssd, j0, CONV_HEAD) for j0 in range(0, n_conv_slabs, slabs_per_dot)]
        tasks += [slab_task(0, 0, j0, POOL_HEAD) for j0 in range(0, n_pool_slabs, slabs_per_dot)]
        tasks += [gate_task(c0) for c0 in range(0, d_ssd, MXU_COLS)]
        return tasks

    def out_project_tasks(r0, mixed):
        d_model = wout_ref.shape[1]

        def block_task(c0):
            def run():
                hpre_ref[:, c0:c0 + MXU_COLS] = _dot(mixed[...], wout_ref[:, c0:c0 + MXU_COLS])
            return run

        def norm_task(ci):
            def run():
                rows = slice(ci * CHUNK, (ci + 1) * CHUNK)
                pre = alpha * xprev_ref[r0 + ci * CHUNK:r0 + (ci + 1) * CHUNK, :] + hpre_ref[rows, :]
                h_ref[r0 + ci * CHUNK:r0 + (ci + 1) * CHUNK, :] = _layer_norm(pre, g_ref[...], b_ref[...])
            return run

        return [block_task(c0) for c0 in range(0, d_model, MXU_COLS)] + [norm_task(ci) for ci in range(CHUNKS_PER_HALF)]

    def mix_chunk(chunk, cur, prev, mixed, m0, fill):
        ubuf, zbuf, xbuf, dtbuf = cur
        mrows = slice(m0, m0 + CHUNK)
        seq_chunk = lax.rem(chunk + chunks_per_seq, chunks_per_seq)
        is_start = seq_chunk == 0
        ubuf[:, 0:POOL_HEAD, :] = jnp.where(is_start, 0.0, prev[0][:, CHUNK:CHUNK + POOL_HEAD, :])
        xbuf[:, 0:CONV_HEAD, :] = jnp.where(is_start, 0.0, prev[2][:, CHUNK:CHUNK + CONV_HEAD, :])

        def conv_silu(j):
            cols = slice(j * LANES, (j + 1) * LANES)
            acc = convb_ref[:, cols] + xbuf[j, CONV_HEAD:CONV_HEAD + CHUNK, :] * convw_ref[CONV_WIDTH - 1:CONV_WIDTH, cols]
            for k in range(CONV_WIDTH - 1):
                off = CONV_HEAD - (CONV_WIDTH - 1) + k
                acc = acc + xbuf[j, off:off + CHUNK, :] * convw_ref[k:k + 1, cols]
            return _silu(acc)

        pos = seq_chunk * CHUNK + row_i + 1
        for gi, w in enumerate(POOL_WINDOWS):
            cur_u = ubuf[gi, POOL_HEAD:POOL_HEAD + CHUNK, :]
            win_sum = cur_u
            for j in range(1, w):
                win_sum = win_sum + ubuf[gi, POOL_HEAD - j:POOL_HEAD - j + CHUNK, :]
            cnt = jnp.minimum(pos, w).astype(F32)
            diff = win_sum / cnt - cur_u
            yp = _dot(diff.astype(BF16), wpool_ref[gi]) * pscale_ref[:, gi * LANES:(gi + 1) * LANES]
            mixed[mrows, gi * LANES:(gi + 1) * LANES] = yp.astype(BF16)

        dt_all = dtbuf[...]
        a = dt_all * a_neg2
        a_hi = a.astype(BF16)
        a_lo = (a - a_hi.astype(F32)).astype(BF16)
        acs = _dot(tri_ref[...], jnp.concatenate([a_hi, a_lo], axis=0))
        packed = acs + jnp.where(lane >= DT_DUP_LANE, dt_all, 0.0)
        packed_t = packed.T
        acs_t = packed_t[0:DT_DUP_LANE]
        dt_t = packed_t[DT_DUP_LANE:2 * DT_DUP_LANE]
        w_t = jnp.exp2(acs_t[:, CHUNK - 1:CHUNK] - acs_t) * dt_t
        src_t = acs_t - jnp.log2(dt_t)

        for g in range(N_BC_GROUPS):
            b_g = conv_silu(x_slabs + g)
            c_g = conv_silu(x_slabs + N_BC_GROUPS + g)
            b_bf = b_g.astype(BF16)
            c_bf = c_g.astype(BF16)
            b_t = b_g.T
            cb = lax.dot_general(c_bf, b_bf, (((1,), (1,)), ((), ())), preferred_element_type=F32)
            gcol = g * group_width
            st_g = jnp.where(is_start, 0.0, state[:, gcol:gcol + group_width])
            y_off = _dot(c_bf, st_g.astype(BF16))
            fill.step()
            v_slabs = []
            ssq = jnp.zeros((CHUNK, 1), F32)
            for q in range(slabs_per_group):
                col0 = gcol + q * LANES
                h0 = g * heads_per_group + q * HEADS_PER_SLAB
                xq = conv_silu(g * slabs_per_group + q)
                xq_bf = xq.astype(BF16)
                zero = jnp.zeros_like(xq_bf)
                rhs = jnp.concatenate([jnp.where(lo_half, xq_bf, zero), jnp.where(lo_half, zero, xq_bf)], axis=0)
                m_parts, bw_parts, e_parts = [], [], []
                for h in (h0, h0 + 1):
                    m_strips, e_strips = [], []
                    for r in range(0, CHUNK, CHUNK // 2):
                        rows = slice(r, r + CHUNK // 2)
                        bc = jnp.broadcast_to(acs[rows, h:h + 1], (CHUNK // 2, LANES))
                        dec_dt = jnp.exp2(jnp.where(causal[rows], bc - src_t[h:h + 1, :], -jnp.inf))
                        m_strips.append((cb[rows] * dec_dt).astype(BF16))
                        e_strips.append(jnp.exp2(bc))
                    m_parts.append(jnp.concatenate(m_strips, axis=0))
                    bw_parts.append((b_t * w_t[h:h + 1, :]).astype(BF16))
                    e_parts.append(jnp.concatenate(e_strips, axis=0))
                lhs = jnp.concatenate([jnp.concatenate(m_parts, axis=1), jnp.concatenate(bw_parts, axis=1)], axis=0)
                res = _dot(lhs, rhs)
                e_pair = jnp.where(lo_half, e_parts[0], e_parts[1])
                y = res[0:CHUNK] + y_off[:, q * LANES:(q + 1) * LANES] * e_pair + dskip_ref[:, col0:col0 + LANES] * xq
                state[:, col0:col0 + LANES] = (st_g[:, q * LANES:(q + 1) * LANES] * e_pair[CHUNK - 1:CHUNK, :]
                                               + res[CHUNK:2 * CHUNK])
                v = y * _silu(zbuf[:, col0:col0 + LANES])
                ssq = ssq + jnp.sum(v * v, axis=-1, keepdims=True)
                v_slabs.append(v)
                fill.step()
            scale = lax.rsqrt(ssq * (1.0 / group_width) + RMS_EPS)
            for q in range(slabs_per_group):
                col0 = gcol + q * LANES
                vn = v_slabs[q] * scale * normw_ref[:, col0:col0 + LANES]
                mixed[mrows, d_pool + col0:d_pool + col0 + LANES] = vn.astype(BF16)

    points_per_chunk = N_BC_GROUPS * (1 + slabs_per_group)
    first = CHUNKS_PER_STEP * t
    for half in range(2):
        r0 = half * half_rows
        new_sets = [sets[half * CHUNKS_PER_HALF + i] for i in range(CHUNKS_PER_HALF)]
        out_mixed = mixed_refs[half]
        fill = _Interleaver(out_project_tasks(r0, out_mixed) + project_tasks(r0, new_sets),
                            CHUNKS_PER_HALF * points_per_chunk)
        for i in range(CHUNKS_PER_HALF):
            local = (half * CHUNKS_PER_HALF - CHUNKS_PER_HALF + i) % CHUNKS_PER_STEP
            chunk = first + half * CHUNKS_PER_HALF - CHUNKS_PER_HALF + i
            mix_chunk(chunk, sets[local], sets[(local - 1) % CHUNKS_PER_STEP],
                      mixed_refs[local // CHUNKS_PER_HALF], (local % CHUNKS_PER_HALF) * CHUNK, fill)
        fill.drain()


def _ffn_kernel(h_ref, w1_ref, w2_ref, g_ref, b_ref, o_ref, *, alpha, ff_block):
    h = h_ref[...]
    hb = h.astype(BF16)
    d_ff = w1_ref.shape[1]
    acc = alpha * h
    for j in range(d_ff // ff_block):
        hid = _dot(hb, w1_ref[:, j * ff_block:(j + 1) * ff_block])
        act = jnp.square(jnp.maximum(hid, 0.0)).astype(BF16)
        acc = acc + _dot(act, w2_ref[j * ff_block:(j + 1) * ff_block, :])
    o_ref[...] = _layer_norm(acc, g_ref[...], b_ref[...])


def _resident(shape):
    nd = len(shape)
    return pl.BlockSpec(shape, lambda *_: (0,) * nd, pipeline_mode=pl.Buffered(1))


def _mixer_layer(x, w_in, w_pool, pool_scale, conv_w, conv_b, dt_bias, a_log, d_skip, ssd_norm_w,
                 w_out, ln_g, ln_b, *, alpha):
    bsz, seq, d_model = x.shape
    n_tok = bsz * seq
    step_rows = CHUNKS_PER_STEP * CHUNK
    half_rows = CHUNKS_PER_HALF * CHUNK
    n_blocks = n_tok // step_rows
    d_pool = w_pool.shape[0] * w_pool.shape[1]
    n_heads = dt_bias.shape[0]
    d_ssd = n_heads * SSD_HEAD_DIM
    d_conv = conv_w.shape[1]
    d_main = d_pool + d_ssd + d_conv
    d_mix = d_pool + d_ssd
    assert w_in.shape[1] == d_main + n_heads and d_conv == d_ssd + 2 * N_BC_GROUPS * D_STATE
    assert n_heads <= DT_DUP_LANE and DT_DUP_LANE + n_heads <= LANES
    assert n_tok % step_rows == 0 and seq % CHUNK == 0 and w_pool.shape[1] == LANES

    x2d = x.reshape(n_tok, d_model)
    w_main = w_in.astype(BF16)
    w_dt = w_in[:, d_main:]
    w_dt_pad = jnp.zeros((d_model, LANES), F32)
    w_dt_pad = w_dt_pad.at[:, :n_heads].set(w_dt).at[:, DT_DUP_LANE:DT_DUP_LANE + n_heads].set(w_dt).astype(BF16)
    dtb_pad = jnp.zeros((1, LANES), F32).at[0, :n_heads].set(dt_bias).at[0, DT_DUP_LANE:DT_DUP_LANE + n_heads].set(dt_bias)
    alog_pad = jnp.zeros((1, LANES), F32).at[0, :n_heads].set(a_log)
    dskip_row = jnp.repeat(d_skip, SSD_HEAD_DIM)[None, :]
    tri = (jnp.arange(CHUNK)[:, None] >= (jnp.arange(2 * CHUNK)[None, :] % CHUNK)).astype(BF16)

    kern = functools.partial(_mixer_kernel, chunks_per_seq=seq // CHUNK, d_pool=d_pool, d_ssd=d_ssd,
                             n_heads=n_heads, alpha=alpha)
    row = lambda v: v[None, :]

    def chunk_set():
        return [
            pltpu.VMEM((d_pool // LANES, POOL_HEAD + CHUNK, LANES), F32),
            pltpu.VMEM((CHUNK, d_ssd), F32),
            pltpu.VMEM((d_conv // LANES, CONV_HEAD + CHUNK, LANES), F32),
            pltpu.VMEM((CHUNK, LANES), F32),
        ]

    scratch = []
    for _ in range(CHUNKS_PER_STEP):
        scratch += chunk_set()
    scratch += [
        pltpu.VMEM((half_rows, d_mix), BF16),
        pltpu.VMEM((half_rows, d_mix), BF16),
        pltpu.VMEM((D_STATE, d_ssd), F32),
        pltpu.VMEM((half_rows, d_model), BF16),
        pltpu.VMEM((half_rows, d_model), F32),
    ]

    return pl.pallas_call(
        kern,
        grid=(n_blocks + 1,),
        in_specs=[
            pl.BlockSpec((step_rows, d_model), lambda t: (jnp.minimum(t, n_blocks - 1), 0)),
            pl.BlockSpec((step_rows, d_model), lambda t: (jnp.maximum(t - 1, 0), 0)),
            _resident(w_main.shape),
            _resident((d_model, LANES)),
            _resident((CONV_WIDTH, d_conv)),
            _resident((1, d_conv)),
            _resident((1, LANES)),
            _resident((1, LANES)),
            _resident((1, d_ssd)),
            _resident((1, d_ssd)),
            _resident(w_pool.shape),
            _resident((1, d_pool)),
            _resident((d_mix, d_model)),
            _resident((1, d_model)),
            _resident((1, d_model)),
            _resident((CHUNK, 2 * CHUNK)),
        ],
        out_specs=pl.BlockSpec((step_rows, d_model), lambda t: (jnp.maximum(t - 1, 0), 0)),
        out_shape=jax.ShapeDtypeStruct((n_tok, d_model), F32),
        scratch_shapes=scratch,
        compiler_params=pltpu.CompilerParams(
            dimension_semantics=("arbitrary",), vmem_limit_bytes=VMEM_LIMIT_BYTES),
        name="mixer",
    )(x2d, x2d, w_main, w_dt_pad, conv_w, row(conv_b), dtb_pad, alog_pad, dskip_row, row(ssd_norm_w),
      w_pool.astype(BF16), row(pool_scale), w_out.astype(BF16), row(ln_g), row(ln_b), tri)


def _ffn_layer(h, w1, w2, ln_g, ln_b, *, alpha, tm, ff_block):
    n_tok, d_model = h.shape
    d_ff = w1.shape[1]
    assert n_tok % tm == 0 and d_ff % ff_block == 0
    kern = functools.partial(_ffn_kernel, alpha=alpha, ff_block=ff_block)
    return pl.pallas_call(
        kern,
        grid=(n_tok // tm,),
        in_specs=[
            pl.BlockSpec((tm, d_model), lambda i: (i, 0)),
            _resident((d_model, d_ff)),
            _resident((d_ff, d_model)),
            _resident((1, d_model)),
            _resident((1, d_model)),
        ],
        out_specs=pl.BlockSpec((tm, d_model), lambda i: (i, 0)),
        out_shape=jax.ShapeDtypeStruct(h.shape, F32),
        compiler_params=pltpu.CompilerParams(
            dimension_semantics=("arbitrary",), vmem_limit_bytes=VMEM_LIMIT_BYTES),
        name="ffn",
    )(h, w1.astype(BF16), w2.astype(BF16), ln_g[None, :], ln_b[None, :])


def kernel(x, w_in, w_pool, pool_scale, conv_w, conv_b, dt_bias, a_log, d_skip, ssd_norm_w, w_out,
           ln1_g, ln1_b, w_ff1, w_ff2, ln2_g, ln2_b):
    depth = w_in.shape[0]
    alpha = (2.0 * depth) ** 0.25
    bsz, seq, d_model = x.shape
    h = x
    for l in range(depth):
        h = _mixer_layer(h.reshape(bsz, seq, d_model), w_in[l], w_pool[l], pool_scale[l], conv_w[l], conv_b[l],
                         dt_bias[l], a_log[l], d_skip[l], ssd_norm_w[l], w_out[l], ln1_g[l], ln1_b[l], alpha=alpha)
        h = _ffn_layer(h, w_ff1[l], w_ff2[l], ln2_g[l], ln2_b[l], alpha=alpha, tm=FFN_ROWS, ff_block=FFN_COLS)
    return h.reshape(bsz, seq, d_model)
```
